```python
import jax, jax.numpy as jnp
from jax import lax
import numpy as np

D_MODEL = 1024
BATCH = 16
SEQ = 4096
DEPTH = 2

GRID_W = 64
EPS = 1e-6
D_MIX = D_MODEL
NA_HEAD_DIM = 64
NA_WIDTH = D_MIX // 2
NA_HEADS = NA_WIDTH // NA_HEAD_DIM
NA_KH = 8
NA_KW = 16
DN_HEAD_DIM = 128
DN_WIDTH = D_MIX - NA_WIDTH
DN_HEADS = DN_WIDTH // DN_HEAD_DIM
DN_CONV = 5
DN_CHUNK = 64
IN_SIZES = (NA_WIDTH,) * 4 + (DN_WIDTH,) * 4 + (DN_HEADS,) * 4
D_IN = sum(IN_SIZES)

kernel_name = 'hymba_natten_gdn_encoder'


def _rmsnorm(x, w):
    xf = x.astype(jnp.float32)
    y = xf * lax.rsqrt(jnp.mean(xf * xf, axis=-1, keepdims=True) + EPS) * w.astype(jnp.float32)
    return y.astype(x.dtype)


def _l2norm(t):
    return t * lax.rsqrt(jnp.sum(t * t, axis=-1, keepdims=True) + EPS)


def _split_proj(p):
    offsets = [int(o) for o in np.cumsum(IN_SIZES)[:-1]]
    return jnp.split(p, offsets, axis=-1)


def _neighbourhood_attention(q, k, v, rpb):
    B, L, H, Dh = q.shape
    rows = L // GRID_W
    kh = min(NA_KH, rows)
    to_grid = lambda t: t.reshape(B, rows, GRID_W, H, Dh).transpose(0, 3, 1, 2, 4)
    qg, kg, vg = to_grid(q), to_grid(k), to_grid(v)
    cols = np.arange(GRID_W)
    cs = np.clip(cols - NA_KW // 2, 0, GRID_W - NA_KW)
    col_idx = cs[:, None] + np.arange(NA_KW)[None, :]
    col_bias = rpb[:, :, col_idx - cols[:, None] + NA_KW - 1]
    scale = Dh ** -0.5

    def row_fn(r):
        rs = jnp.clip(r - kh // 2, 0, rows - kh)
        q_r = lax.dynamic_index_in_dim(qg, r, axis=2, keepdims=False)
        k_sel = jnp.take(lax.dynamic_slice_in_dim(kg, rs, kh, axis=2), col_idx, axis=3)
        v_sel = jnp.take(lax.dynamic_slice_in_dim(vg, rs, kh, axis=2), col_idx, axis=3)
        dr = rs + jnp.arange(kh) - r + NA_KH - 1
        bias = jnp.take(col_bias, dr, axis=1).transpose(0, 2, 1, 3)
        s = jnp.einsum('bhwd,bhiwjd->bhwij', q_r, k_sel).astype(jnp.float32) * scale + bias.astype(jnp.float32)
        p = jax.nn.softmax(s.reshape(B, H, GRID_W, kh * NA_KW), axis=-1).reshape(s.shape).astype(v.dtype)
        return jnp.einsum('bhwij,bhiwjd->bhwd', p, v_sel)

    o = lax.map(row_fn, jnp.arange(rows))
    return o.transpose(1, 0, 3, 2, 4).reshape(B, L, H * Dh)


def _centred_depthwise_conv(x, w):
    c = x.shape[-1]
    return lax.conv_general_dilated(
        x, w[:, None, :].astype(x.dtype), window_strides=(1,),
        padding=[(DN_CONV // 2, DN_CONV // 2)],
        dimension_numbers=('NWC', 'WIO', 'NWC'), feature_group_count=c)


def _gated_delta_chunked(q, k, v, g, beta):
    B, H, L, Dk = q.shape
    Dv = v.shape[-1]
    C = DN_CHUNK
    N = L // C
    q = q * (Dk ** -0.5)
    ch = lambda t: t.reshape((B, H, N, C) + t.shape[3:])
    q, k, v, g, beta = ch(q), ch(k), ch(v), ch(g), ch(beta)
    g = jnp.cumsum(g, axis=-1)
    tri = np.tril(np.ones((C, C), dtype=bool))
    strict = np.tril(np.ones((C, C), dtype=bool), -1)
    diff = g[..., :, None] - g[..., None, :]
    decay = jnp.where(tri, jnp.exp(jnp.where(tri, diff, 0.0)), 0.0)
    kb = k * beta[..., None]
    a_mat = jnp.where(strict, jnp.einsum('bhncd,bhnsd->bhncs', kb, k) * decay, 0.0) + jnp.eye(C, dtype=jnp.float32)
    rhs = jnp.concatenate([v * beta[..., None], kb * jnp.exp(g)[..., None]], axis=-1)
    sol = lax.linalg.triangular_solve(a_mat, rhs, left_side=True, lower=True, unit_diagonal=True)
    u, w = sol[..., :Dv], sol[..., Dv:]
    intra = jnp.einsum('bhncd,bhnsd->bhncs', q, k) * decay

    def step(S, inp):
        q_c, k_c, u_c, w_c, g_c, a_c = inp
        v_new = u_c - jnp.einsum('bhcd,bhde->bhce', w_c, S)
        o = jnp.einsum('bhcd,bhde->bhce', q_c * jnp.exp(g_c)[..., None], S) + jnp.einsum('bhcs,bhse->bhce', a_c, v_new)
        g_last = g_c[..., -1]
        S = S * jnp.exp(g_last)[..., None, None] + jnp.einsum(
            'bhcd,bhce->bhde', k_c * jnp.exp(g_last[..., None] - g_c)[..., None], v_new)
        return S, o

    xs = tuple(jnp.moveaxis(t, 2, 0) for t in (q, k, u, w, g, intra))
    S0 = jnp.zeros((B, H, Dk, Dv), jnp.float32)
    _, o = lax.scan(step, S0, xs)
    return jnp.moveaxis(o, 0, 2).reshape(B, H, L, Dv)


def _layer(x, norm_w, w_in, qk_gain_q, qk_gain_k, rpb, conv_w, a_log, dt_bias, dn_norm_w, w_out):
    B, L, _ = x.shape
    h = _rmsnorm(x, norm_w)
    proj = jnp.einsum('bld,de->ble', h, w_in)
    aq, ak, av, az, dq, dk, dv, dz, b_f, a_f, b_b, a_b = _split_proj(proj)
    heads = lambda t, n: t.reshape(B, L, n, -1)

    aq = _rmsnorm(heads(aq, NA_HEADS), qk_gain_q)
    ak = _rmsnorm(heads(ak, NA_HEADS), qk_gain_k)
    attn = _neighbourhood_attention(aq, ak, heads(av, NA_HEADS), rpb) * jax.nn.silu(az)

    qkv = jax.nn.silu(_centred_depthwise_conv(jnp.concatenate([dq, dk, dv], axis=-1), conv_w))
    dq, dk, dv = jnp.split(qkv, 3, axis=-1)
    to_bhld = lambda t: heads(t, DN_HEADS).astype(jnp.float32).transpose(0, 2, 1, 3)
    dq, dk, dv = _l2norm(to_bhld(dq)), _l2norm(to_bhld(dk)), to_bhld(dv)

    def gates(b, a, d):
        beta = jax.nn.sigmoid(b.astype(jnp.float32))
        g = -jnp.exp(a_log[d].astype(jnp.float32)) * jax.nn.softplus(a.astype(jnp.float32) + dt_bias[d].astype(jnp.float32))
        return g.transpose(0, 2, 1), beta.transpose(0, 2, 1)

    g_f, beta_f = gates(b_f, a_f, 0)
    g_b, beta_b = gates(b_b, a_b, 1)
    flip = lambda t: jnp.flip(t, axis=2)
    o_f = _gated_delta_chunked(dq, dk, dv, g_f, beta_f)
    o_b = flip(_gated_delta_chunked(flip(dq), flip(dk), flip(dv), flip(g_b), flip(beta_b)))
    o = (o_f + o_b).transpose(0, 2, 1, 3)
    o = o * lax.rsqrt(jnp.mean(o * o, axis=-1, keepdims=True) + EPS) * dn_norm_w.astype(jnp.float32)
    o = o * jax.nn.silu(heads(dz, DN_HEADS).astype(jnp.float32))
    delta = o.reshape(B, L, DN_WIDTH).astype(x.dtype)

    y = jnp.einsum('ble,ed->bld', jnp.concatenate([attn, delta], axis=-1), w_out)
    return x + y


def setup_inputs(seed: int = 0) -> dict:
    key = jax.random.key(seed)
    ks = jax.random.split(key, 12)
    f32 = jnp.float32
    x = jax.random.normal(ks[0], (BATCH, SEQ, D_MODEL), f32)
    norm_w = 1.0 + 0.02 * jax.random.normal(ks[1], (DEPTH, D_MODEL), f32)
    w_in = jax.random.normal(ks[2], (DEPTH, D_MODEL, D_IN), f32) * D_MODEL ** -0.5
    qk_gain_q = 1.0 + 0.02 * jax.random.normal(ks[3], (DEPTH, NA_HEAD_DIM), f32)
    qk_gain_k = 1.0 + 0.02 * jax.random.normal(ks[4], (DEPTH, NA_HEAD_DIM), f32)
    rpb = 0.02 * jax.random.normal(ks[5], (DEPTH, NA_HEADS, 2 * NA_KH - 1, 2 * NA_KW - 1), f32)
    conv_w = jax.random.normal(ks[6], (DEPTH, DN_CONV, 3 * DN_WIDTH), f32) * DN_CONV ** -0.5
    a_log = jnp.log(jax.random.uniform(ks[7], (DEPTH, 2, DN_HEADS), f32, 1.0, 16.0))
    u = jax.random.uniform(ks[8], (DEPTH, 2, DN_HEADS), f32)
    dt = jnp.exp(u * (jnp.log(0.1) - jnp.log(0.001)) + jnp.log(0.001))
    dt_bias = dt + jnp.log(-jnp.expm1(-dt))
    dn_norm_w = 1.0 + 0.02 * jax.random.normal(ks[9], (DEPTH, DN_HEAD_DIM), f32)
    w_out = jax.random.normal(ks[10], (DEPTH, D_MIX, D_MODEL), f32) * D_MIX ** -0.5
    return {'x': x, 'norm_w': norm_w, 'w_in': w_in, 'qk_gain_q': qk_gain_q, 'qk_gain_k': qk_gain_k,
            'rpb': rpb, 'conv_w': conv_w, 'a_log': a_log, 'dt_bias': dt_bias,
            'dn_norm_w': dn_norm_w, 'w_out': w_out}


def reference(x, norm_w, w_in, qk_gain_q, qk_gain_k, rpb, conv_w, a_log, dt_bias, dn_norm_w, w_out):
    for l in range(DEPTH):
        x = _layer(x, norm_w[l], w_in[l], qk_gain_q[l], qk_gain_k[l], rpb[l], conv_w[l],
                   a_log[l], dt_bias[l], dn_norm_w[l], w_out[l])
    return x
```

```python
import functools

import jax
import jax.numpy as jnp
import numpy as np
from jax import lax
from jax.experimental import pallas as pl
from jax.experimental.pallas import tpu as pltpu

F32 = jnp.float32
BF16 = jnp.bfloat16

D_MODEL = 1024
GRID_W = 64
EPS = 1e-6
NA_HEAD_DIM = 64
NA_WIDTH = 512
NA_HEADS = 8
NA_KH = 8
NA_KW = 16
DN_HEAD_DIM = 128
DN_WIDTH = 512
DN_HEADS = 4
DN_CONV = 5
N_GATE = 4 * DN_HEADS

LANES = 128
CHUNK = 128
NEG = -1e30
VMEM_LIMIT = 56 * 1024 * 1024

N_FEAT = 4
F_BETA_F, F_GC_F, F_BETA_B, F_GC_B = 0, 1, 2, 3


def _dot(a, b):
    return jnp.dot(a, b, preferred_element_type=F32)


def _dot_nt(a, b):
    return lax.dot_general(a, b, (((1,), (1,)), ((), ())), preferred_element_type=F32)


def _dot_tn(a, b):
    return lax.dot_general(a, b, (((0,), (0,)), ((), ())), preferred_element_type=F32)


def _sigmoid(x):
    return 1.0 / (1.0 + jnp.exp(-x))


def _silu(x):
    return x * _sigmoid(x)


TM_IN = 512


def _inproj_kernel(x_ref, nw_ref, w_ref, wg_ref, gq_ref, gk_ref, seg_ref, alog_ref, dtb_ref,
                   aq_ref, ak_ref, av_ref, az_ref, dqkv_ref, dz_ref, gf_ref, gt_ref):
    x = x_ref[...]
    ms = jnp.mean(x * x, axis=-1, keepdims=True)
    h = (x * lax.rsqrt(ms + EPS) * nw_ref[...]).astype(BF16)

    def proj(c0, n):
        return _dot(h, w_ref[:, c0:c0 + n])

    def head_rmsnorm(t, gain_ref):
        sq = t * t
        hi = sq.astype(BF16)
        lo = (sq - hi.astype(F32)).astype(BF16)
        seg = seg_ref[...]
        m = (_dot(hi, seg) + _dot(lo, seg)) * (1.0 / NA_HEAD_DIM)
        return t * lax.rsqrt(m + EPS) * gain_ref[...]

    aq_ref[...] = head_rmsnorm(proj(0, NA_WIDTH), gq_ref).astype(BF16)
    ak_ref[...] = head_rmsnorm(proj(NA_WIDTH, NA_WIDTH), gk_ref).astype(BF16)
    av_ref[...] = proj(2 * NA_WIDTH, NA_WIDTH).astype(BF16)
    az_ref[...] = proj(3 * NA_WIDTH, NA_WIDTH)
    c0 = 4 * NA_WIDTH
    for i in range(3):
        dqkv_ref[:, i * DN_WIDTH:(i + 1) * DN_WIDTH] = proj(c0 + i * DN_WIDTH, DN_WIDTH)
    dz_ref[...] = proj(c0 + 3 * DN_WIDTH, DN_WIDTH)

    raw = _dot(h, wg_ref[...])
    tm = raw.shape[0]
    lane = lax.broadcasted_iota(jnp.int32, raw.shape, 1)
    pos = lax.broadcasted_iota(jnp.int32, raw.shape, 0) % CHUNK
    beta = _sigmoid(raw)
    z = raw + dtb_ref[...]
    softplus = jnp.maximum(z, 0.0) + jnp.log1p(jnp.exp(-jnp.abs(z)))
    g = -jnp.exp(alog_ref[...]) * softplus
    cf = g
    cb = g
    s = 1
    while s < CHUNK:
        cf = cf + jnp.where(pos >= s, pltpu.roll(cf, s, 0), 0.0)
        cb = cb + jnp.where(pos < CHUNK - s, pltpu.roll(cb, tm - s, 0), 0.0)
        s *= 2
    f_idx = lane % N_FEAT
    feat = jnp.where(f_idx == F_GC_F, cf, jnp.where(f_idx == F_GC_B, cb, beta))
    feat = jnp.where(lane < N_GATE, feat, 0.0)
    gf_ref[...] = feat
    gt_ref[...] = feat.T[:N_GATE, :]


def _inproj(x2d, nw, w_main, w_gate, gq, gk, seg, alog_vec, dtb_vec):
    t = x2d.shape[0]
    tm = TM_IN
    grid = (t // tm,)
    const = lambda i: (0, 0)
    row = lambda i: (i, 0)
    out_shape = (
        jax.ShapeDtypeStruct((t, NA_WIDTH), BF16),
        jax.ShapeDtypeStruct((t, NA_WIDTH), BF16),
        jax.ShapeDtypeStruct((t, NA_WIDTH), BF16),
        jax.ShapeDtypeStruct((t, NA_WIDTH), F32),
        jax.ShapeDtypeStruct((t, 3 * DN_WIDTH), F32),
        jax.ShapeDtypeStruct((t, DN_WIDTH), F32),
        jax.ShapeDtypeStruct((t, LANES), F32),
        jax.ShapeDtypeStruct((N_GATE, t), F32),
    )
    in_specs = [
        pl.BlockSpec((tm, D_MODEL), row),
        pl.BlockSpec((1, D_MODEL), const),
        pl.BlockSpec(w_main.shape, const),
        pl.BlockSpec(w_gate.shape, const),
        pl.BlockSpec((1, NA_WIDTH), const),
        pl.BlockSpec((1, NA_WIDTH), const),
        pl.BlockSpec((NA_WIDTH, NA_WIDTH), const),
        pl.BlockSpec((1, LANES), const),
        pl.BlockSpec((1, LANES), const),
    ]
    out_specs = (
        pl.BlockSpec((tm, NA_WIDTH), row),
        pl.BlockSpec((tm, NA_WIDTH), row),
        pl.BlockSpec((tm, NA_WIDTH), row),
        pl.BlockSpec((tm, NA_WIDTH), row),
        pl.BlockSpec((tm, 3 * DN_WIDTH), row),
        pl.BlockSpec((tm, DN_WIDTH), row),
        pl.BlockSpec((tm, LANES), row),
        pl.BlockSpec((N_GATE, tm), lambda i: (0, i)),
    )
    return pl.pallas_call(
        _inproj_kernel, grid=grid, in_specs=in_specs, out_specs=out_specs, out_shape=out_shape,
        compiler_params=pltpu.CompilerParams(dimension_semantics=("arbitrary",),
                                             vmem_limit_bytes=VMEM_LIMIT),
        name="inproj",
    )(x2d, nw, w_main, w_gate, gq, gk, seg, alog_vec, dtb_vec)


NA_GROUP = 4
NA_GW = NA_GROUP * NA_HEAD_DIM
NA_KEYS = NA_KH * GRID_W
N_DR = 2 * NA_KH - 2


def _natten_kernel(q_ref, k_ref, v_ref, z_ref, bias_ref, o_ref):
    rows = q_ref.shape[0] // GRID_W
    lane_head = lax.broadcasted_iota(jnp.int32, (GRID_W, NA_GW), 1) // NA_HEAD_DIM

    def row_fn(r, carry):
        rs = jnp.clip(r - NA_KH // 2, 0, rows - NA_KH)
        q0 = pl.multiple_of(r * GRID_W, GRID_W)
        k0 = pl.multiple_of(rs * GRID_W, GRID_W)
        q = q_ref[pl.ds(q0, GRID_W), :]
        kw = k_ref[pl.ds(k0, NA_KEYS), :]
        vw = v_ref[pl.ds(k0, NA_KEYS), :]
        zero = jnp.zeros_like(q)
        q4 = jnp.concatenate([jnp.where(lane_head == hh, q, zero) for hh in range(NA_GROUP)], axis=0)
        s = _dot_nt(q4, kw)
        dr0 = rs - r + (NA_KH - 1)
        bias = jnp.concatenate(
            [jnp.concatenate([bias_ref[hh, dr0 + 2 * j] for j in range(NA_KH // 2)], axis=1)
             for hh in range(NA_GROUP)], axis=0)
        s = s + bias
        m = jnp.max(s, axis=-1, keepdims=True)
        p = jnp.exp(s - m)
        l = jnp.sum(p, axis=-1, keepdims=True)
        o = _dot(p.astype(BF16), vw) * (1.0 / l)
        out = jnp.zeros((GRID_W, NA_GW), F32)
        for hh in range(NA_GROUP):
            out = jnp.where(lane_head == hh, o[hh * GRID_W:(hh + 1) * GRID_W, :], out)
        z = z_ref[pl.ds(q0, GRID_W), :]
        o_ref[pl.ds(q0, GRID_W), :] = (out * _silu(z)).astype(o_ref.dtype)
        return carry

    lax.fori_loop(0, rows, row_fn, 0)


def _natten(aq, ak, av, az, bias_tab, batch, seq):
    n_groups = NA_HEADS // NA_GROUP
    blk = lambda b, g: (b, g)
    spec = pl.BlockSpec((seq, NA_GW), blk)
    return pl.pallas_call(
        _natten_kernel,
        grid=(batch, n_groups),
        in_specs=[spec, spec, spec, spec,
                  pl.BlockSpec((NA_GROUP, N_DR, GRID_W, LANES), lambda b, g: (g, 0, 0, 0))],
        out_specs=spec,
        out_shape=jax.ShapeDtypeStruct((batch * seq, NA_WIDTH), BF16),
        compiler_params=pltpu.CompilerParams(dimension_semantics=("arbitrary", "arbitrary"),
                                             vmem_limit_bytes=VMEM_LIMIT),
        name="natten",
    )(aq, ak, av, az, bias_tab)


def _bias_table(rpb):
    cols = np.arange(GRID_W)
    cs = np.clip(cols - NA_KW // 2, 0, GRID_W - NA_KW)
    kc = cols[None, :]
    c = cols[:, None]
    valid = (kc >= cs[:, None]) & (kc < cs[:, None] + NA_KW)
    idx = np.clip(kc - c + NA_KW - 1, 0, 2 * NA_KW - 2)
    tab = jnp.where(valid[None, None], rpb[:, :, idx], NEG)
    return jnp.concatenate([tab[:, :-1], tab[:, 1:]], axis=-1).astype(F32)


CONV_TILE = 256
CONV_HALO = 8
N_SQUARINGS = 6
PREP_UNROLL = 2


def _gdn_kernel(xq_ref, xk_ref, xv_ref, z_ref, gf_ref, gt_ref, cwq_ref, cwk_ref, cwv_ref, nw_ref,
                o_ref,
                xpad, qn, kn, vn, qp, kw, rr, eg, oacc):
    seq = xq_ref.shape[0]
    n_chunks = seq // CHUNK
    h = pl.program_id(1)

    xpad[0:CONV_HALO, :] = jnp.zeros((CONV_HALO, LANES), F32)
    xpad[CONV_HALO + seq:2 * CONV_HALO + seq, :] = jnp.zeros((CONV_HALO, LANES), F32)

    def conv_pass(x_ref, cw_ref, dst, l2_scale):
        def copy_fn(i, c):
            r0 = pl.multiple_of(i * CONV_TILE, CONV_TILE)
            xpad[pl.ds(r0 + CONV_HALO, CONV_TILE), :] = x_ref[pl.ds(r0, CONV_TILE), :]
            return c
        lax.fori_loop(0, seq // CONV_TILE, copy_fn, 0)
        cw = cw_ref[...]

        def tile_fn(i, c):
            r0 = pl.multiple_of(i * CONV_TILE, CONV_TILE)
            win = xpad[pl.ds(r0, CONV_TILE + 2 * CONV_HALO), :]
            y = jnp.zeros((CONV_TILE, LANES), F32)
            for j in range(DN_CONV):
                off = CONV_HALO - DN_CONV // 2 + j
                y = y + win[off:off + CONV_TILE, :] * cw[j:j + 1, :]
            y = _silu(y)
            if l2_scale is not None:
                y = y * (lax.rsqrt(jnp.sum(y * y, axis=-1, keepdims=True) + EPS) * l2_scale)
            dst[pl.ds(r0, CONV_TILE), :] = y
            return c
        lax.fori_loop(0, seq // CONV_TILE, tile_fn, 0)

    conv_pass(xq_ref, cwq_ref, qn, DN_HEAD_DIM ** -0.5)
    conv_pass(xk_ref, cwk_ref, kn, 1.0)
    conv_pass(xv_ref, cwv_ref, vn, None)

    ii = lax.broadcasted_iota(jnp.int32, (CHUNK, CHUNK), 0)
    jj = lax.broadcasted_iota(jnp.int32, (CHUNK, CHUNK), 1)
    shift = (LANES - h * N_FEAT) % LANES

    def prep_chunk(c):
        r0 = pl.multiple_of(c * CHUNK, CHUNK)
        rows = pl.ds(r0, CHUNK)
        q = qn[rows, :]
        k = kn[rows, :]
        v = vn[rows, :]
        k16 = k.astype(BF16)
        kk0 = _dot_nt(k16, k16)
        qk0 = _dot_nt(q.astype(BF16), k16)
        gfr = pltpu.roll(gf_ref[rows, :], shift, 1)
        osum = None
        for d, (fb, fg) in enumerate(((F_BETA_F, F_GC_F), (F_BETA_B, F_GC_B))):
            beta = gfr[:, fb:fb + 1]
            gc = gfr[:, fg:fg + 1]
            gcr = gt_ref[fg:fg + 1, rows]
            if d == 0:
                incl, strict = jj <= ii, jj < ii
                glast = gc[CHUNK - 1:CHUNK, :]
            else:
                incl, strict = jj >= ii, jj > ii
                glast = gc[0:1, :]
            decay = jnp.exp(jnp.where(incl, gc - gcr, NEG))
            m = jnp.where(strict, -(beta * kk0 * decay), 0.0)
            e = m
            pw = m
            for _ in range(N_SQUARINGS):
                pw16 = pw.astype(BF16)
                pw = _dot(pw16, pw16)
                e = e + pw + _dot(e.astype(BF16), pw.astype(BF16))
            egc = jnp.exp(gc)
            rhs = jnp.concatenate([v * beta, k * (beta * egc)], axis=1)
            sol = rhs + _dot(e.astype(BF16), rhs.astype(BF16))
            sol16 = sol.astype(BF16)
            intra = (qk0 * decay).astype(BF16)
            qo = _dot(intra, sol16)
            kdec = (k * jnp.exp(glast - gc)).astype(BF16)
            tr = _dot_tn(kdec, sol16)
            qp[d, rows, :] = (q * egc - qo[:, DN_HEAD_DIM:]).astype(BF16)
            kw[d, rows, :] = tr[:, DN_HEAD_DIM:].astype(BF16)
            rr[d, rows, :] = tr[:, :DN_HEAD_DIM]
            eg[d, pl.ds(pl.multiple_of(c * 8, 8), 8), :] = jnp.broadcast_to(jnp.exp(glast), (8, LANES))
            osum = qo[:, :DN_HEAD_DIM] if osum is None else osum + qo[:, :DN_HEAD_DIM]
        oacc[rows, :] = osum

    def prep_fn(i, c):
        for u in range(PREP_UNROLL):
            prep_chunk(i * PREP_UNROLL + u)
        return c
    lax.fori_loop(0, n_chunks // PREP_UNROLL, prep_fn, 0)

    def scan_fn(s, carry):
        new = []
        for d, c in ((0, s), (1, n_chunks - 1 - s)):
            st = carry[d]
            st16 = st.astype(BF16)
            r0 = pl.multiple_of(c * CHUNK, CHUNK)
            rows = pl.ds(r0, CHUNK)
            oacc[rows, :] += _dot(qp[d, rows, :], st16)
            decay = eg[d, pl.ds(pl.multiple_of(c * 8, 8), 1), :]
            new.append(st * decay - _dot(kw[d, rows, :], st16) + rr[d, rows, :])
        return tuple(new)

    zero = jnp.zeros((DN_HEAD_DIM, DN_HEAD_DIM), F32)
    lax.fori_loop(0, n_chunks, scan_fn, (zero, zero))

    def out_fn(i, c):
        r0 = pl.multiple_of(i * CONV_TILE, CONV_TILE)
        rows = pl.ds(r0, CONV_TILE)
        o = oacc[rows, :]
        o = o * lax.rsqrt(jnp.mean(o * o, axis=-1, keepdims=True) + EPS) * nw_ref[...]
        o_ref[rows, :] = (o * _silu(z_ref[rows, :])).astype(o_ref.dtype)
        return c
    lax.fori_loop(0, seq // CONV_TILE, out_fn, 0)


def _gdn(dqkv, dz, gf, gt, conv_w, dn_w, batch, seq):
    tok = lambda off: pl.BlockSpec((seq, LANES), lambda b, h: (b, h + off))
    cw = lambda off: pl.BlockSpec((DN_CONV, LANES), lambda b, h: (0, h + off))
    scratch = [
        pltpu.VMEM((seq + 2 * CONV_HALO, LANES), F32),
        pltpu.VMEM((seq, LANES), F32),
        pltpu.VMEM((seq, LANES), F32),
        pltpu.VMEM((seq, LANES), F32),
        pltpu.VMEM((2, seq, LANES), BF16),
        pltpu.VMEM((2, seq, LANES), BF16),
        pltpu.VMEM((2, seq, LANES), F32),
        pltpu.VMEM((2, seq // CHUNK * 8, LANES), F32),
        pltpu.VMEM((seq, LANES), F32),
    ]
    return pl.pallas_call(
        _gdn_kernel,
        grid=(batch, DN_HEADS),
        in_specs=[tok(0), tok(DN_HEADS), tok(2 * DN_HEADS),
                  pl.BlockSpec((seq, LANES), lambda b, h: (b, h)),
                  pl.BlockSpec((seq, LANES), lambda b, h: (b, 0)),
                  pl.BlockSpec((None, N_FEAT, seq), lambda b, h: (h, 0, b)),
                  cw(0), cw(DN_HEADS), cw(2 * DN_HEADS),
                  pl.BlockSpec((1, LANES), lambda b, h: (0, 0))],
        out_specs=pl.BlockSpec((seq, LANES), lambda b, h: (b, h)),
        out_shape=jax.ShapeDtypeStruct((batch * seq, DN_WIDTH), BF16),
        scratch_shapes=scratch,
        compiler_params=pltpu.CompilerParams(dimension_semantics=("arbitrary", "arbitrary"),
                                             vmem_limit_bytes=VMEM_LIMIT),
        name="gdn",
    )(dqkv, dqkv, dqkv, dz, gf, gt.reshape(DN_HEADS, N_FEAT, -1), conv_w, conv_w, conv_w, dn_w)


TM_OUT = 512


def _outproj_kernel(a_ref, d_ref, wa_ref, wd_ref, x_ref, o_ref):
    o_ref[...] = x_ref[...] + _dot(a_ref[...], wa_ref[...]) + _dot(d_ref[...], wd_ref[...])


def _outproj(attn, delta, w_a, w_d, x2d):
    t = x2d.shape[0]
    tm = TM_OUT
    row = lambda i: (i, 0)
    const = lambda i: (0, 0)
    return pl.pallas_call(
        _outproj_kernel,
        grid=(t // tm,),
        in_specs=[pl.BlockSpec((tm, NA_WIDTH), row), pl.BlockSpec((tm, DN_WIDTH), row),
                  pl.BlockSpec(w_a.shape, const), pl.BlockSpec(w_d.shape, const),
                  pl.BlockSpec((tm, D_MODEL), row)],
        out_specs=pl.BlockSpec((tm, D_MODEL), row),
        out_shape=jax.ShapeDtypeStruct((t, D_MODEL), F32),
        compiler_params=pltpu.CompilerParams(dimension_semantics=("arbitrary",),
                                             vmem_limit_bytes=VMEM_LIMIT),
        name="outproj",
    )(attn, delta, w_a, w_d, x2d)


def _gate_lane_vector(p):
    v = jnp.zeros((DN_HEADS, N_FEAT), F32)
    v = v.at[:, F_GC_F].set(p[0].astype(F32)).at[:, F_GC_B].set(p[1].astype(F32))
    return jnp.pad(v.reshape(1, N_GATE), ((0, 0), (0, LANES - N_GATE)))


def _layer(x2d, batch, seq, norm_w, w_in, gain_q, gain_k, rpb, conv_w, a_log, dt_bias, dn_norm_w, w_out):
    n_main = 4 * NA_WIDTH + 4 * DN_WIDTH
    w_main = w_in[:, :n_main].astype(BF16)
    w_gate = w_in[:, n_main:].reshape(-1, N_FEAT, DN_HEADS).transpose(0, 2, 1).reshape(-1, N_GATE)
    w_gate = jnp.pad(w_gate, ((0, 0), (0, LANES - N_GATE))).astype(BF16)
    gq = jnp.tile(gain_q.astype(F32) * NA_HEAD_DIM ** -0.5, NA_HEADS)[None, :]
    gk = jnp.tile(gain_k.astype(F32), NA_HEADS)[None, :]
    head_of = np.arange(NA_WIDTH) // NA_HEAD_DIM
    seg = jnp.asarray(head_of[:, None] == head_of[None, :], BF16)
    aq, ak, av, az, dqkv, dz, gf, gt = _inproj(
        x2d, norm_w[None, :].astype(F32), w_main, w_gate, gq, gk, seg,
        _gate_lane_vector(a_log), _gate_lane_vector(dt_bias))
    attn = _natten(aq, ak, av, az, _bias_table(rpb), batch, seq)
    delta = _gdn(dqkv, dz, gf, gt, conv_w.astype(F32), dn_norm_w[None, :].astype(F32), batch, seq)
    w_o = w_out.astype(BF16)
    return _outproj(attn, delta, w_o[:NA_WIDTH], w_o[NA_WIDTH:], x2d)


def kernel(x, norm_w, w_in, qk_gain_q, qk_gain_k, rpb, conv_w, a_log, dt_bias, dn_norm_w, w_out):
    batch, seq, d = x.shape
    x2d = x.reshape(batch * seq, d)
    for l in range(norm_w.shape[0]):
        x2d = _layer(x2d, batch, seq, norm_w[l], w_in[l], qk_gain_q[l], qk_gain_k[l], rpb[l],
                     conv_w[l], a_log[l], dt_bias[l], dn_norm_w[l], w_out[l])
    return x2d.reshape(batch, seq, d)
```

```python
import functools

import jax
import jax.numpy as jnp
import numpy as np
from jax import lax
from jax.experimental import pallas as pl
from jax.experimental.pallas import tpu as pltpu

F32 = jnp.float32
BF16 = jnp.bfloat16

D_MODEL = 1024
GRID_W = 64
EPS = 1e-6
NA_HEAD_DIM = 64
NA_WIDTH = 512
NA_HEADS = 8
NA_KH = 8
NA_KW = 16
DN_HEAD_DIM = 128
DN_WIDTH = 512
DN_HEADS = 4
DN_CONV = 5
N_GATE = 4 * DN_HEADS

LANES = 128
CHUNK = 128
NEG = -1e30
VMEM_LIMIT = 56 * 1024 * 1024

N_FEAT = 4
F_BETA_F, F_GC_F, F_BETA_B, F_GC_B = 0, 1, 2, 3


def _dot(a, b):
    return jnp.dot(a, b, preferred_element_type=F32)


def _dot_nt(a, b):
    return lax.dot_general(a, b, (((1,), (1,)), ((), ())), preferred_element_type=F32)


def _dot_tn(a, b):
    return lax.dot_general(a, b, (((0,), (0,)), ((), ())), preferred_element_type=F32)


def _sigmoid(x):
    return 1.0 / (1.0 + jnp.exp(-x))


def _silu(x):
    return x * _sigmoid(x)


TM_IN = 512


def _inproj_kernel(x_ref, nw_ref, w_ref, wg_ref, gq_ref, gk_ref, seg_ref, alog_ref, dtb_ref,
                   aq_ref, ak_ref, av_ref, az_ref, dqkv_ref, dz_ref, gf_ref, gt_ref):
    x = x_ref[...]
    ms = jnp.mean(x * x, axis=-1, keepdims=True)
    h = (x * lax.rsqrt(ms + EPS) * nw_ref[...]).astype(BF16)

    def proj(c0, n):
        return _dot(h, w_ref[:, c0:c0 + n])

    def head_rmsnorm(t, gain_ref):
        sq = t * t
        hi = sq.astype(BF16)
        lo = (sq - hi.astype(F32)).astype(BF16)
        seg = seg_ref[...]
        m = (_dot(hi, seg) + _dot(lo, seg)) * (1.0 / NA_HEAD_DIM)
        return t * lax.rsqrt(m + EPS) * gain_ref[...]

    aq_ref[...] = head_rmsnorm(proj(0, NA_WIDTH), gq_ref).astype(BF16)
    ak_ref[...] = head_rmsnorm(proj(NA_WIDTH, NA_WIDTH), gk_ref).astype(BF16)
    av_ref[...] = proj(2 * NA_WIDTH, NA_WIDTH).astype(BF16)
    az_ref[...] = proj(3 * NA_WIDTH, NA_WIDTH)
    c0 = 4 * NA_WIDTH
    for i in range(3):
        dqkv_ref[:, i * DN_WIDTH:(i + 1) * DN_WIDTH] = proj(c0 + i * DN_WIDTH, DN_WIDTH)
    dz_ref[...] = proj(c0 + 3 * DN_WIDTH, DN_WIDTH)

    raw = _dot(h, wg_ref[...])
    tm = raw.shape[0]
    lane = lax.broadcasted_iota(jnp.int32, raw.shape, 1)
    pos = lax.broadcasted_iota(jnp.int32, raw.shape, 0) % CHUNK
    beta = _sigmoid(raw)
    z = raw + dtb_ref[...]
    softplus = jnp.maximum(z, 0.0) + jnp.log1p(jnp.exp(-jnp.abs(z)))
    g = -jnp.exp(alog_ref[...]) * softplus
    cf = g
    cb = g
    s = 1
    while s < CHUNK:
        cf = cf + jnp.where(pos >= s, pltpu.roll(cf, s, 0), 0.0)
        cb = cb + jnp.where(pos < CHUNK - s, pltpu.roll(cb, tm - s, 0), 0.0)
        s *= 2
    f_idx = lane % N_FEAT
    feat = jnp.where(f_idx == F_GC_F, cf, jnp.where(f_idx == F_GC_B, cb, beta))
    feat = jnp.where(lane < N_GATE, feat, 0.0)
    gf_ref[...] = feat
    gt_ref[...] = feat.T[:N_GATE, :]


def _inproj(x2d, nw, w_main, w_gate, gq, gk, seg, alog_vec, dtb_vec):
    t = x2d.shape[0]
    tm = TM_IN
    grid = (t // tm,)
    const = lambda i: (0, 0)
    row = lambda i: (i, 0)
    out_shape = (
        jax.ShapeDtypeStruct((t, NA_WIDTH), BF16),
        jax.ShapeDtypeStruct((t, NA_WIDTH), BF16),
        jax.ShapeDtypeStruct((t, NA_WIDTH), BF16),
        jax.ShapeDtypeStruct((t, NA_WIDTH), F32),
        jax.ShapeDtypeStruct((t, 3 * DN_WIDTH), F32),
        jax.ShapeDtypeStruct((t, DN_WIDTH), F32),
        jax.ShapeDtypeStruct((t, LANES), F32),
        jax.ShapeDtypeStruct((N_GATE, t), F32),
    )
    in_specs = [
        pl.BlockSpec((tm, D_MODEL), row),
        pl.BlockSpec((1, D_MODEL), const),
        pl.BlockSpec(w_main.shape, const),
        pl.BlockSpec(w_gate.shape, const),
        pl.BlockSpec((1, NA_WIDTH), const),
        pl.BlockSpec((1, NA_WIDTH), const),
        pl.BlockSpec((NA_WIDTH, NA_WIDTH), const),
        pl.BlockSpec((1, LANES), const),
        pl.BlockSpec((1, LANES), const),
    ]
    out_specs = (
        pl.BlockSpec((tm, NA_WIDTH), row),
        pl.BlockSpec((tm, NA_WIDTH), row),
        pl.BlockSpec((tm, NA_WIDTH), row),
        pl.BlockSpec((tm, NA_WIDTH), row),
        pl.BlockSpec((tm, 3 * DN_WIDTH), row),
        pl.BlockSpec((tm, DN_WIDTH), row),
        pl.BlockSpec((tm, LANES), row),
        pl.BlockSpec((N_GATE, tm), lambda i: (0, i)),
    )
    return pl.pallas_call(
        _inproj_kernel, grid=grid, in_specs=in_specs, out_specs=out_specs, out_shape=out_shape,
        compiler_params=pltpu.CompilerParams(dimension_semantics=("arbitrary",),
                                             vmem_limit_bytes=VMEM_LIMIT),
        name="inproj",
    )(x2d, nw, w_main, w_gate, gq, gk, seg, alog_vec, dtb_vec)


NA_GROUP = 4
NA_GW = NA_GROUP * NA_HEAD_DIM
NA_KEYS = NA_KH * GRID_W
N_DR = 2 * NA_KH - 2
NA_ROWS = 4


def _natten_kernel(q_ref, k_ref, v_ref, z_ref, bias_ref, o_ref):
    rows = q_ref.shape[0] // GRID_W
    lane_head = lax.broadcasted_iota(jnp.int32, (GRID_W, NA_GW), 1) // NA_HEAD_DIM

    def rows_fn(i, carry):
        rr = [i * NA_ROWS + u for u in range(NA_ROWS)]
        rs = [jnp.clip(r - NA_KH // 2, 0, rows - NA_KH) for r in rr]
        q0 = [pl.multiple_of(r * GRID_W, GRID_W) for r in rr]
        k0 = [pl.multiple_of(x * GRID_W, GRID_W) for x in rs]
        s = []
        for u in range(NA_ROWS):
            q = q_ref[pl.ds(q0[u], GRID_W), :]
            zero = jnp.zeros_like(q)
            q4 = jnp.concatenate([jnp.where(lane_head == hh, q, zero) for hh in range(NA_GROUP)], axis=0)
            s.append(_dot_nt(q4, k_ref[pl.ds(k0[u], NA_KEYS), :]))
        p, linv = [], []
        for u in range(NA_ROWS):
            dr0 = rs[u] - rr[u] + (NA_KH - 1)
            bias = jnp.concatenate(
                [jnp.concatenate([bias_ref[hh, dr0 + 2 * j] for j in range(NA_KH // 2)], axis=1)
                 for hh in range(NA_GROUP)], axis=0)
            sb = s[u] + bias
            e = jnp.exp(sb - jnp.max(sb, axis=-1, keepdims=True))
            linv.append(1.0 / jnp.sum(e, axis=-1, keepdims=True))
            p.append(e.astype(BF16))
        o = [_dot(p[u], v_ref[pl.ds(k0[u], NA_KEYS), :]) for u in range(NA_ROWS)]
        for u in range(NA_ROWS):
            on = o[u] * linv[u]
            out = jnp.zeros((GRID_W, NA_GW), F32)
            for hh in range(NA_GROUP):
                out = jnp.where(lane_head == hh, on[hh * GRID_W:(hh + 1) * GRID_W, :], out)
            z = z_ref[pl.ds(q0[u], GRID_W), :]
            o_ref[pl.ds(q0[u], GRID_W), :] = (out * _silu(z)).astype(o_ref.dtype)
        return carry

    lax.fori_loop(0, rows // NA_ROWS, rows_fn, 0)


def _natten(aq, ak, av, az, bias_tab, batch, seq):
    n_groups = NA_HEADS // NA_GROUP
    blk = lambda b, g: (b, g)
    spec = pl.BlockSpec((seq, NA_GW), blk)
    return pl.pallas_call(
        _natten_kernel,
        grid=(batch, n_groups),
        in_specs=[spec, spec, spec, spec,
                  pl.BlockSpec((NA_GROUP, N_DR, GRID_W, LANES), lambda b, g: (g, 0, 0, 0))],
        out_specs=spec,
        out_shape=jax.ShapeDtypeStruct((batch * seq, NA_WIDTH), BF16),
        compiler_params=pltpu.CompilerParams(dimension_semantics=("arbitrary", "arbitrary"),
                                             vmem_limit_bytes=VMEM_LIMIT),
        name="natten",
    )(aq, ak, av, az, bias_tab)


def _bias_table(rpb):
    cols = np.arange(GRID_W)
    cs = np.clip(cols - NA_KW // 2, 0, GRID_W - NA_KW)
    kc = cols[None, :]
    c = cols[:, None]
    valid = (kc >= cs[:, None]) & (kc < cs[:, None] + NA_KW)
    idx = np.clip(kc - c + NA_KW - 1, 0, 2 * NA_KW - 2)
    tab = jnp.where(valid[None, None], rpb[:, :, idx], NEG)
    return jnp.concatenate([tab[:, :-1], tab[:, 1:]], axis=-1).astype(F32)


CONV_TILE = 256
CONV_HALO = 8
N_SQUARINGS = 6
PREP_UNROLL = 8


def _gdn_kernel(xq_ref, xk_ref, xv_ref, z_ref, gf_ref, gt_ref, cwq_ref, cwk_ref, cwv_ref, nw_ref,
                o_ref,
                xpad, qn, kn, vn, qp, kw, rr, eg, oacc):
    seq = xq_ref.shape[0]
    n_chunks = seq // CHUNK
    h = pl.program_id(1)

    n_tiles = seq // CONV_TILE

    def conv_pass(x_ref, cw_ref, dst, l2_scale):
        def copy_fn(i, c):
            r0 = pl.multiple_of(i * CONV_TILE, CONV_TILE)
            lo = pl.multiple_of(jnp.maximum(r0 - CONV_HALO, 0), CONV_HALO)
            hi = pl.multiple_of(jnp.minimum(r0 + CONV_TILE, seq - CONV_HALO), CONV_HALO)
            xpad[i, CONV_HALO:CONV_HALO + CONV_TILE, :] = x_ref[pl.ds(r0, CONV_TILE), :]
            xpad[i, 0:CONV_HALO, :] = jnp.where(i > 0, x_ref[pl.ds(lo, CONV_HALO), :], 0.0)
            xpad[i, CONV_HALO + CONV_TILE:, :] = jnp.where(i < n_tiles - 1, x_ref[pl.ds(hi, CONV_HALO), :], 0.0)
            return c
        lax.fori_loop(0, n_tiles, copy_fn, 0)
        cw = cw_ref[...]

        def tile_fn(i, c):
            r0 = pl.multiple_of(i * CONV_TILE, CONV_TILE)
            y = jnp.zeros((CONV_TILE, LANES), F32)
            for j in range(DN_CONV):
                off = CONV_HALO - DN_CONV // 2 + j
                y = y + xpad[i, off:off + CONV_TILE, :] * cw[j:j + 1, :]
            y = _silu(y)
            if l2_scale is not None:
                y = y * (lax.rsqrt(jnp.sum(y * y, axis=-1, keepdims=True) + EPS) * l2_scale)
            dst[pl.ds(r0, CONV_TILE), :] = y
            return c
        lax.fori_loop(0, n_tiles, tile_fn, 0)

    conv_pass(xq_ref, cwq_ref, qn, DN_HEAD_DIM ** -0.5)
    conv_pass(xk_ref, cwk_ref, kn, 1.0)
    conv_pass(xv_ref, cwv_ref, vn, None)

    ii = lax.broadcasted_iota(jnp.int32, (CHUNK, CHUNK), 0)
    jj = lax.broadcasted_iota(jnp.int32, (CHUNK, CHUNK), 1)
    shift = (LANES - h * N_FEAT) % LANES

    dirs = ((F_BETA_F, F_GC_F), (F_BETA_B, F_GC_B))

    def prep_group(cs):
        n = len(cs)
        chains = [(u, d) for u in range(n) for d in range(2)]
        rows_of = [pl.ds(pl.multiple_of(c * CHUNK, CHUNK), CHUNK) for c in cs]
        k32 = [kn[r, :] for r in rows_of]
        k16 = [k.astype(BF16) for k in k32]
        kk0 = [_dot_nt(k, k) for k in k16]
        gfr = [pltpu.roll(gf_ref[r, :], shift, 1) for r in rows_of]
        bb, bg, glast, decay, m = [], [], [], [], []
        for u, d in chains:
            fb, fg = dirs[d]
            bb.append(jnp.broadcast_to(gfr[u][:, fb:fb + 1], (CHUNK, LANES)))
            g = jnp.broadcast_to(gfr[u][:, fg:fg + 1], (CHUNK, LANES))
            bg.append(g)
            glast.append(g[CHUNK - 1:CHUNK, :] if d == 0 else g[0:1, :])
            gcr = gt_ref[fg:fg + 1, rows_of[u]]
            incl = (jj <= ii) if d == 0 else (jj >= ii)
            strict = (jj < ii) if d == 0 else (jj > ii)
            decay.append(jnp.exp(jnp.where(incl, g - gcr, NEG)))
            m.append(jnp.where(strict, -(bb[-1] * kk0[u] * decay[-1]), 0.0))
        e = list(m)
        pw16 = [p.astype(BF16) for p in m]
        for _ in range(N_SQUARINGS):
            pw = [_dot(p, p) for p in pw16]
            pw16 = [p.astype(BF16) for p in pw]
            ep = [_dot(x.astype(BF16), p) for x, p in zip(e, pw16)]
            e = [x + p + y for x, p, y in zip(e, pw, ep)]
        sols = []
        for (u, d), x, b, g in zip(chains, e, bb, bg):
            rhs = jnp.concatenate([vn[rows_of[u], :] * b, k32[u] * (b * jnp.exp(g))], axis=1)
            sols.append((rhs + _dot(x.astype(BF16), rhs.astype(BF16))).astype(BF16))
        q32 = [qn[r, :] for r in rows_of]
        qk0 = [_dot_nt(q.astype(BF16), k) for q, k in zip(q32, k16)]
        qos, trs = [], []
        for (u, d), sol16, g, gl, dc in zip(chains, sols, bg, glast, decay):
            qos.append(_dot((qk0[u] * dc).astype(BF16), sol16))
            kdec = (k32[u] * jnp.exp(gl - g)).astype(BF16)
            trs.append(_dot_tn(kdec, sol16))
        for (u, d), qo, tr, g, gl in zip(chains, qos, trs, bg, glast):
            rows = rows_of[u]
            qp[d, rows, :] = (q32[u] * jnp.exp(g) - qo[:, DN_HEAD_DIM:]).astype(BF16)
            kw[d, rows, :] = tr[:, DN_HEAD_DIM:].astype(BF16)
            rr[d, rows, :] = tr[:, :DN_HEAD_DIM]
            eg[d, pl.ds(pl.multiple_of(cs[u] * 8, 8), 8), :] = jnp.broadcast_to(jnp.exp(gl), (8, LANES))
            if d == 0:
                oacc[rows, :] = qo[:, :DN_HEAD_DIM]
            else:
                oacc[rows, :] += qo[:, :DN_HEAD_DIM]

    def prep_fn(i, c):
        prep_group([i * PREP_UNROLL + u for u in range(PREP_UNROLL)])
        return c
    lax.fori_loop(0, n_chunks // PREP_UNROLL, prep_fn, 0)

    def scan_fn(s, carry):
        new = []
        for d, c in ((0, s), (1, n_chunks - 1 - s)):
            st = carry[d]
            st16 = st.astype(BF16)
            r0 = pl.multiple_of(c * CHUNK, CHUNK)
            rows = pl.ds(r0, CHUNK)
            oacc[rows, :] += _dot(qp[d, rows, :], st16)
            decay = eg[d, pl.ds(pl.multiple_of(c * 8, 8), 1), :]
            new.append(st * decay - _dot(kw[d, rows, :], st16) + rr[d, rows, :])
        return tuple(new)

    zero = jnp.zeros((DN_HEAD_DIM, DN_HEAD_DIM), F32)
    lax.fori_loop(0, n_chunks, scan_fn, (zero, zero))

    def out_fn(i, c):
        r0 = pl.multiple_of(i * CONV_TILE, CONV_TILE)
        rows = pl.ds(r0, CONV_TILE)
        o = oacc[rows, :]
        o = o * lax.rsqrt(jnp.mean(o * o, axis=-1, keepdims=True) + EPS) * nw_ref[...]
        o_ref[rows, :] = (o * _silu(z_ref[rows, :])).astype(o_ref.dtype)
        return c
    lax.fori_loop(0, seq // CONV_TILE, out_fn, 0)


def _gdn(dqkv, dz, gf, gt, conv_w, dn_w, batch, seq):
    tok = lambda off: pl.BlockSpec((seq, LANES), lambda b, h: (b, h + off))
    cw = lambda off: pl.BlockSpec((DN_CONV, LANES), lambda b, h: (0, h + off))
    scratch = [
        pltpu.VMEM((seq // CONV_TILE, CONV_TILE + 2 * CONV_HALO, LANES), F32),
        pltpu.VMEM((seq, LANES), F32),
        pltpu.VMEM((seq, LANES), F32),
        pltpu.VMEM((seq, LANES), F32),
        pltpu.VMEM((2, seq, LANES), BF16),
        pltpu.VMEM((2, seq, LANES), BF16),
        pltpu.VMEM((2, seq, LANES), F32),
        pltpu.VMEM((2, seq // CHUNK * 8, LANES), F32),
        pltpu.VMEM((seq, LANES), F32),
    ]
    return pl.pallas_call(
        _gdn_kernel,
        grid=(batch, DN_HEADS),
        in_specs=[tok(0), tok(DN_HEADS), tok(2 * DN_HEADS),
                  pl.BlockSpec((seq, LANES), lambda b, h: (b, h)),
                  pl.BlockSpec((seq, LANES), lambda b, h: (b, 0)),
                  pl.BlockSpec((None, N_FEAT, seq), lambda b, h: (h, 0, b)),
                  cw(0), cw(DN_HEADS), cw(2 * DN_HEADS),
                  pl.BlockSpec((1, LANES), lambda b, h: (0, 0))],
        out_specs=pl.BlockSpec((seq, LANES), lambda b, h: (b, h)),
        out_shape=jax.ShapeDtypeStruct((batch * seq, DN_WIDTH), BF16),
        scratch_shapes=scratch,
        compiler_params=pltpu.CompilerParams(dimension_semantics=("arbitrary", "arbitrary"),
                                             vmem_limit_bytes=VMEM_LIMIT),
        name="gdn",
    )(dqkv, dqkv, dqkv, dz, gf, gt.reshape(DN_HEADS, N_FEAT, -1), conv_w, conv_w, conv_w, dn_w)


TM_OUT = 512


def _outproj_kernel(a_ref, d_ref, wa_ref, wd_ref, x_ref, o_ref):
    o_ref[...] = x_ref[...] + _dot(a_ref[...], wa_ref[...]) + _dot(d_ref[...], wd_ref[...])


def _outproj(attn, delta, w_a, w_d, x2d):
    t = x2d.shape[0]
    tm = TM_OUT
    row = lambda i: (i, 0)
    const = lambda i: (0, 0)
    return pl.pallas_call(
        _outproj_kernel,
        grid=(t // tm,),
        in_specs=[pl.BlockSpec((tm, NA_WIDTH), row), pl.BlockSpec((tm, DN_WIDTH), row),
                  pl.BlockSpec(w_a.shape, const), pl.BlockSpec(w_d.shape, const),
                  pl.BlockSpec((tm, D_MODEL), row)],
        out_specs=pl.BlockSpec((tm, D_MODEL), row),
        out_shape=jax.ShapeDtypeStruct((t, D_MODEL), F32),
        compiler_params=pltpu.CompilerParams(dimension_semantics=("arbitrary",),
                                             vmem_limit_bytes=VMEM_LIMIT),
        name="outproj",
    )(attn, delta, w_a, w_d, x2d)


def _gate_lane_vector(p):
    v = jnp.zeros((DN_HEADS, N_FEAT), F32)
    v = v.at[:, F_GC_F].set(p[0].astype(F32)).at[:, F_GC_B].set(p[1].astype(F32))
    return jnp.pad(v.reshape(1, N_GATE), ((0, 0), (0, LANES - N_GATE)))


def _layer(x2d, batch, seq, norm_w, w_in, gain_q, gain_k, rpb, conv_w, a_log, dt_bias, dn_norm_w, w_out):
    n_main = 4 * NA_WIDTH + 4 * DN_WIDTH
    w_main = w_in[:, :n_main].astype(BF16)
    w_gate = w_in[:, n_main:].reshape(-1, N_FEAT, DN_HEADS).transpose(0, 2, 1).reshape(-1, N_GATE)
    w_gate = jnp.pad(w_gate, ((0, 0), (0, LANES - N_GATE))).astype(BF16)
    gq = jnp.tile(gain_q.astype(F32) * NA_HEAD_DIM ** -0.5, NA_HEADS)[None, :]
    gk = jnp.tile(gain_k.astype(F32), NA_HEADS)[None, :]
    head_of = np.arange(NA_WIDTH) // NA_HEAD_DIM
    seg = jnp.asarray(head_of[:, None] == head_of[None, :], BF16)
    aq, ak, av, az, dqkv, dz, gf, gt = _inproj(
        x2d, norm_w[None, :].astype(F32), w_main, w_gate, gq, gk, seg,
        _gate_lane_vector(a_log), _gate_lane_vector(dt_bias))
    attn = _natten(aq, ak, av, az, _bias_table(rpb), batch, seq)
    delta = _gdn(dqkv, dz, gf, gt, conv_w.astype(F32), dn_norm_w[None, :].astype(F32), batch, seq)
    w_o = w_out.astype(BF16)
    return _outproj(attn, delta, w_o[:NA_WIDTH], w_o[NA_WIDTH:], x2d)


def kernel(x, norm_w, w_in, qk_gain_q, qk_gain_k, rpb, conv_w, a_log, dt_bias, dn_norm_w, w_out):
    batch, seq, d = x.shape
    x2d = x.reshape(batch * seq, d)
    for l in range(norm_w.shape[0]):
        x2d = _layer(x2d, batch, seq, norm_w[l], w_in[l], qk_gain_q[l], qk_gain_k[l], rpb[l],
                     conv_w[l], a_log[l], dt_bias[l], dn_norm_w[l], w_out[l])
    return x2d.reshape(batch, seq, d)
```

```python
import functools

import jax
import jax.numpy as jnp
import numpy as np
from jax import lax
from jax.experimental import pallas as pl
from jax.experimental.pallas import tpu as pltpu

F32 = jnp.float32
BF16 = jnp.bfloat16

D_MODEL = 1024
GRID_W = 64
EPS = 1e-6
NA_HEAD_DIM = 64
NA_WIDTH = 512
NA_HEADS = 8
NA_KH = 8
NA_KW = 16
DN_HEAD_DIM = 128
DN_WIDTH = 512
DN_HEADS = 4
DN_CONV = 5
N_GATE = 4 * DN_HEADS

LANES = 128
CHUNK = 128
NEG = -1e30
VMEM_LIMIT = 56 * 1024 * 1024

N_FEAT = 4
F_BETA_F, F_GC_F, F_BETA_B, F_GC_B = 0, 1, 2, 3


def _dot(a, b):
    return jnp.dot(a, b, preferred_element_type=F32)


def _dot_nt(a, b):
    return lax.dot_general(a, b, (((1,), (1,)), ((), ())), preferred_element_type=F32)


def _dot_tn(a, b):
    return lax.dot_general(a, b, (((0,), (0,)), ((), ())), preferred_element_type=F32)


def _sigmoid(x):
    return 1.0 / (1.0 + jnp.exp(-x))


def _silu(x):
    return x * _sigmoid(x)


TM_IN = 512
HALO_IN = 16


def _inproj_kernel(tiles_per_seq, xp_ref, x_ref, xn_ref, nw_ref, w_ref, wg_ref, gq_ref, gk_ref, seg_ref,
                   alog_ref, dtb_ref, cw_ref,
                   aq_ref, ak_ref, av_ref, az_ref, dqkv_ref, dz_ref, gf_ref, gt_ref,
                   ph_ref):
    def rmsnorm(x):
        ms = jnp.mean(x * x, axis=-1, keepdims=True)
        return (x * lax.rsqrt(ms + EPS) * nw_ref[...]).astype(BF16)

    h = rmsnorm(x_ref[...])

    def proj(c0, n):
        return _dot(h, w_ref[:, c0:c0 + n])

    tm = h.shape[0]
    c_dn = 4 * NA_WIDTH
    h_ext = jnp.concatenate([rmsnorm(xp_ref[...]), h, rmsnorm(xn_ref[...])], axis=0)
    p_ext = _dot(h_ext, w_ref[:, c_dn:c_dn + 3 * DN_WIDTH])
    t_in_seq = pl.program_id(0) % tiles_per_seq
    row = lax.broadcasted_iota(jnp.int32, (tm + 2 * HALO_IN, 1), 0)
    inside = ((row >= HALO_IN) | (t_in_seq > 0)) & ((row < HALO_IN + tm) | (t_in_seq < tiles_per_seq - 1))
    ph_ref[...] = jnp.where(inside, p_ext, 0.0)
    for cb in range(3 * DN_HEADS):
        lanes = slice(cb * LANES, (cb + 1) * LANES)
        y = jnp.zeros((tm, LANES), F32)
        for j in range(DN_CONV):
            off = HALO_IN - DN_CONV // 2 + j
            y = y + ph_ref[off:off + tm, lanes] * cw_ref[j:j + 1, lanes]
        y = _silu(y)
        if cb < 2 * DN_HEADS:
            scale = DN_HEAD_DIM ** -0.5 if cb < DN_HEADS else 1.0
            y = y * (lax.rsqrt(jnp.sum(y * y, axis=-1, keepdims=True) + EPS) * scale)
        dqkv_ref[:, lanes] = y

    def head_rmsnorm(t, gain_ref):
        m = _dot((t * t).astype(BF16), seg_ref[...]) * (1.0 / NA_HEAD_DIM)
        return t * lax.rsqrt(m + EPS) * gain_ref[...]

    aq_ref[...] = head_rmsnorm(proj(0, NA_WIDTH), gq_ref).astype(BF16)
    ak_ref[...] = head_rmsnorm(proj(NA_WIDTH, NA_WIDTH), gk_ref).astype(BF16)
    av_ref[...] = proj(2 * NA_WIDTH, NA_WIDTH).astype(BF16)
    az_ref[...] = proj(3 * NA_WIDTH, NA_WIDTH)
    dz_ref[...] = proj(c_dn + 3 * DN_WIDTH, DN_WIDTH)

    raw = _dot(h, wg_ref[...])
    lane = lax.broadcasted_iota(jnp.int32, raw.shape, 1)
    pos = lax.broadcasted_iota(jnp.int32, raw.shape, 0) % CHUNK
    beta = _sigmoid(raw)
    z = raw + dtb_ref[...]
    softplus = jnp.maximum(z, 0.0) + jnp.log1p(jnp.exp(-jnp.abs(z)))
    g = -jnp.exp(alog_ref[...]) * softplus
    cf = g
    cb = g
    s = 1
    while s < CHUNK:
        cf = cf + jnp.where(pos >= s, pltpu.roll(cf, s, 0), 0.0)
        cb = cb + jnp.where(pos < CHUNK - s, pltpu.roll(cb, tm - s, 0), 0.0)
        s *= 2
    f_idx = lane % N_FEAT
    feat = jnp.where(f_idx == F_GC_F, cf, jnp.where(f_idx == F_GC_B, cb, beta))
    feat = jnp.where(lane < N_GATE, feat, 0.0)
    gf_ref[...] = feat
    gt_ref[...] = feat.T[:N_GATE, :]


def _inproj(x2d, nw, w_main, w_gate, gq, gk, seg, alog_vec, dtb_vec, conv_w, seq):
    t = x2d.shape[0]
    tm = TM_IN
    grid = (t // tm,)
    const = lambda i: (0, 0)
    row = lambda i: (i, 0)
    halo_per_tile = tm // HALO_IN
    last_halo = t // HALO_IN - 1
    prev_rows = lambda i: (jnp.maximum(i * halo_per_tile - 1, 0), 0)
    next_rows = lambda i: (jnp.minimum((i + 1) * halo_per_tile, last_halo), 0)
    out_shape = (
        jax.ShapeDtypeStruct((t, NA_WIDTH), BF16),
        jax.ShapeDtypeStruct((t, NA_WIDTH), BF16),
        jax.ShapeDtypeStruct((t, NA_WIDTH), BF16),
        jax.ShapeDtypeStruct((t, NA_WIDTH), F32),
        jax.ShapeDtypeStruct((t, 3 * DN_WIDTH), F32),
        jax.ShapeDtypeStruct((t, DN_WIDTH), F32),
        jax.ShapeDtypeStruct((t, LANES), F32),
        jax.ShapeDtypeStruct((N_GATE, t), F32),
    )
    in_specs = [
        pl.BlockSpec((HALO_IN, D_MODEL), prev_rows),
        pl.BlockSpec((tm, D_MODEL), row),
        pl.BlockSpec((HALO_IN, D_MODEL), next_rows),
        pl.BlockSpec((1, D_MODEL), const),
        pl.BlockSpec(w_main.shape, const),
        pl.BlockSpec(w_gate.shape, const),
        pl.BlockSpec((1, NA_WIDTH), const),
        pl.BlockSpec((1, NA_WIDTH), const),
        pl.BlockSpec((NA_WIDTH, NA_WIDTH), const),
        pl.BlockSpec((1, LANES), const),
        pl.BlockSpec((1, LANES), const),
        pl.BlockSpec(conv_w.shape, const),
    ]
    out_specs = (
        pl.BlockSpec((tm, NA_WIDTH), row),
        pl.BlockSpec((tm, NA_WIDTH), row),
        pl.BlockSpec((tm, NA_WIDTH), row),
        pl.BlockSpec((tm, NA_WIDTH), row),
        pl.BlockSpec((tm, 3 * DN_WIDTH), row),
        pl.BlockSpec((tm, DN_WIDTH), row),
        pl.BlockSpec((tm, LANES), row),
        pl.BlockSpec((N_GATE, tm), lambda i: (0, i)),
    )
    return pl.pallas_call(
        functools.partial(_inproj_kernel, seq // tm),
        grid=grid, in_specs=in_specs, out_specs=out_specs, out_shape=out_shape,
        scratch_shapes=[pltpu.VMEM((tm + 2 * HALO_IN, 3 * DN_WIDTH), F32)],
        compiler_params=pltpu.CompilerParams(dimension_semantics=("arbitrary",),
                                             vmem_limit_bytes=VMEM_LIMIT),
        name="inproj",
    )(x2d, x2d, x2d, nw, w_main, w_gate, gq, gk, seg, alog_vec, dtb_vec, conv_w)


NA_GROUP = 4
NA_GW = NA_GROUP * NA_HEAD_DIM
NA_KEYS = NA_KH * GRID_W
N_DR = 2 * NA_KH - 2
NA_ROWS = 4


def _natten_kernel(q_ref, k_ref, v_ref, z_ref, bias_ref, o_ref):
    rows = q_ref.shape[0] // GRID_W
    lane_head = lax.broadcasted_iota(jnp.int32, (GRID_W, NA_GW), 1) // NA_HEAD_DIM

    def rows_fn(i, carry):
        rr = [i * NA_ROWS + u for u in range(NA_ROWS)]
        rs = [jnp.clip(r - NA_KH // 2, 0, rows - NA_KH) for r in rr]
        q0 = [pl.multiple_of(r * GRID_W, GRID_W) for r in rr]
        k0 = [pl.multiple_of(x * GRID_W, GRID_W) for x in rs]
        s = []
        for u in range(NA_ROWS):
            q = q_ref[pl.ds(q0[u], GRID_W), :]
            zero = jnp.zeros_like(q)
            q4 = jnp.concatenate([jnp.where(lane_head == hh, q, zero) for hh in range(NA_GROUP)], axis=0)
            s.append(_dot_nt(q4, k_ref[pl.ds(k0[u], NA_KEYS), :]))
        p, linv = [], []
        for u in range(NA_ROWS):
            dr0 = rs[u] - rr[u] + (NA_KH - 1)
            bias = jnp.concatenate(
                [jnp.concatenate([bias_ref[hh, dr0 + 2 * j] for j in range(NA_KH // 2)], axis=1)
                 for hh in range(NA_GROUP)], axis=0)
            sb = s[u] + bias
            e = jnp.exp(sb - jnp.max(sb, axis=-1, keepdims=True))
            linv.append(1.0 / jnp.sum(e, axis=-1, keepdims=True))
            p.append(e.astype(BF16))
        o = [_dot(p[u], v_ref[pl.ds(k0[u], NA_KEYS), :]) for u in range(NA_ROWS)]
        for u in range(NA_ROWS):
            on = o[u] * linv[u]
            out = jnp.zeros((GRID_W, NA_GW), F32)
            for hh in range(NA_GROUP):
                out = jnp.where(lane_head == hh, on[hh * GRID_W:(hh + 1) * GRID_W, :], out)
            z = z_ref[pl.ds(q0[u], GRID_W), :]
            o_ref[pl.ds(q0[u], GRID_W), :] = (out * _silu(z)).astype(o_ref.dtype)
        return carry

    lax.fori_loop(0, rows // NA_ROWS, rows_fn, 0)


def _natten(aq, ak, av, az, bias_tab, batch, seq):
    n_groups = NA_HEADS // NA_GROUP
    blk = lambda b, g: (b, g)
    spec = pl.BlockSpec((seq, NA_GW), blk)
    return pl.pallas_call(
        _natten_kernel,
        grid=(batch, n_groups),
        in_specs=[spec, spec, spec, spec,
                  pl.BlockSpec((NA_GROUP, N_DR, GRID_W, LANES), lambda b, g: (g, 0, 0, 0))],
        out_specs=spec,
        out_shape=jax.ShapeDtypeStruct((batch * seq, NA_WIDTH), BF16),
        compiler_params=pltpu.CompilerParams(dimension_semantics=("arbitrary", "arbitrary"),
                                             vmem_limit_bytes=VMEM_LIMIT),
        name="natten",
    )(aq, ak, av, az, bias_tab)


def _bias_table(rpb):
    cols = np.arange(GRID_W)
    cs = np.clip(cols - NA_KW // 2, 0, GRID_W - NA_KW)
    kc = cols[None, :]
    c = cols[:, None]
    valid = (kc >= cs[:, None]) & (kc < cs[:, None] + NA_KW)
    idx = np.clip(kc - c + NA_KW - 1, 0, 2 * NA_KW - 2)
    tab = jnp.where(valid[None, None], rpb[:, :, idx], NEG)
    return jnp.concatenate([tab[:, :-1], tab[:, 1:]], axis=-1).astype(F32)


OUT_TILE = 256
HALF = CHUNK // 2
N_SQUARINGS = 4
PREP_UNROLL = 8


def _triangular_inverse(ms, directions, tick):
    lo = lax.broadcasted_iota(jnp.int32, (HALF, CHUNK), 1) < HALF

    def bf(x):
        return x.astype(BF16)

    def block_diag(x16):
        z = jnp.zeros_like(x16)
        return jnp.concatenate([jnp.where(lo, x16, z), jnp.where(lo, z, x16)], axis=0)

    def below(x16):
        return jnp.concatenate([jnp.zeros_like(x16), x16], axis=0)

    def above(x16):
        return jnp.concatenate([x16, jnp.zeros_like(x16)], axis=0)

    top = [m[:HALF, :] for m in ms]
    bot = [m[HALF:, :] for m in ms]
    pk = [jnp.where(lo, t, b) for t, b in zip(top, bot)]
    m16 = [bf(p) for p in pk]
    e = list(pk)
    pw16 = m16
    for _ in range(N_SQUARINGS):
        pw = [_dot(p, block_diag(p)) for p in pw16]
        tick()
        pw16 = [bf(p) for p in pw]
        ep = [_dot(bf(x), block_diag(p)) for x, p in zip(e, pw16)]
        tick()
        e = [x + p + y for x, p, y in zip(e, pw, ep)]
    res = [m - x + _dot(a, block_diag(bf(x))) for m, a, x in zip(pk, m16, e)]
    tick()
    er = [_dot(bf(x), block_diag(bf(r))) for x, r in zip(e, res)]
    tick()
    e = [x + r + y for x, r, y in zip(e, res, er)]
    off, t = [], []
    for d, tp, bt, x in zip(directions, top, bot, e):
        if d == 0:
            o = jnp.where(lo, bt, 0.0)
            t.append(o + _dot(bf(jnp.where(lo, 0.0, x)), below(bf(o))))
        else:
            o = jnp.where(lo, 0.0, tp)
            t.append(o + _dot(bf(jnp.where(lo, x, 0.0)), above(bf(o))))
        off.append(o)
    tick()
    out = []
    for d, tt, x in zip(directions, t, e):
        if d == 0:
            eoff = tt + _dot(bf(tt), above(bf(jnp.where(lo, x, 0.0))))
            out.append(jnp.concatenate([jnp.where(lo, x, 0.0), jnp.where(lo, eoff, x)], axis=0))
        else:
            eoff = tt + _dot(bf(tt), below(bf(jnp.where(lo, 0.0, x))))
            out.append(jnp.concatenate([jnp.where(lo, x, eoff), jnp.where(lo, 0.0, x)], axis=0))
    tick()
    return out


def _gdn_kernel(qn, kn, vn, z_ref, gf_ref, gt_ref, nw_ref,
                o_ref,
                qp, kw, rr, eg, oacc):
    seq = qn.shape[0]
    n_chunks = seq // CHUNK
    h = pl.program_id(1)

    ii = lax.broadcasted_iota(jnp.int32, (CHUNK, CHUNK), 0)
    jj = lax.broadcasted_iota(jnp.int32, (CHUNK, CHUNK), 1)
    shift = (LANES - h * N_FEAT) % LANES

    dirs = ((F_BETA_F, F_GC_F), (F_BETA_B, F_GC_B))

    def chunk_rows(c):
        return pl.ds(pl.multiple_of(c * CHUNK, CHUNK), CHUNK)

    def prep_group(chains, tick):
        rows_of = [chunk_rows(c) for c, _ in chains]
        k32 = [kn[r, :] for r in rows_of]
        k16 = [k.astype(BF16) for k in k32]
        kk0 = [_dot_nt(k, k) for k in k16]
        tick()
        bb, bg, glast, decay, m = [], [], [], [], []
        for u, (c, d) in enumerate(chains):
            fb, fg = dirs[d]
            gfr = pltpu.roll(gf_ref[rows_of[u], :], shift, 1)
            bb.append(jnp.broadcast_to(gfr[:, fb:fb + 1], (CHUNK, LANES)))
            g = jnp.broadcast_to(gfr[:, fg:fg + 1], (CHUNK, LANES))
            bg.append(g)
            glast.append(g[CHUNK - 1:CHUNK, :] if d == 0 else g[0:1, :])
            gcr = gt_ref[fg:fg + 1, rows_of[u]]
            incl = (jj <= ii) if d == 0 else (jj >= ii)
            strict = (jj < ii) if d == 0 else (jj > ii)
            decay.append(jnp.exp(jnp.where(incl, g - gcr, NEG)))
            m.append(jnp.where(strict, -(bb[-1] * kk0[u] * decay[-1]), 0.0))
        e = _triangular_inverse(m, [d for _, d in chains], tick)
        sols = []
        for u, (x, b, g) in enumerate(zip(e, bb, bg)):
            rhs = jnp.concatenate([vn[rows_of[u], :] * b, k32[u] * (b * jnp.exp(g))], axis=1)
            sols.append((rhs + _dot(x.astype(BF16), rhs.astype(BF16))).astype(BF16))
        tick()
        q32 = [qn[r, :] for r in rows_of]
        qk0 = [_dot_nt(q.astype(BF16), k) for q, k in zip(q32, k16)]
        tick()
        qos, trs = [], []
        for u, (sol16, g, gl, dc) in enumerate(zip(sols, bg, glast, decay)):
            qos.append(_dot((qk0[u] * dc).astype(BF16), sol16))
            kdec = (k32[u] * jnp.exp(gl - g)).astype(BF16)
            trs.append(_dot_tn(kdec, sol16))
        tick()
        for (c, d), rows, q, qo, tr, g, gl in zip(chains, rows_of, q32, qos, trs, bg, glast):
            qp[d, rows, :] = (q * jnp.exp(g) - qo[:, DN_HEAD_DIM:]).astype(BF16)
            kw[d, rows, :] = tr[:, DN_HEAD_DIM:].astype(BF16)
            rr[d, rows, :] = tr[:, :DN_HEAD_DIM]
            eg[d, pl.ds(pl.multiple_of(c * 8, 8), 8), :] = jnp.broadcast_to(jnp.exp(gl), (8, LANES))
            oacc[rows, :] += qo[:, :DN_HEAD_DIM]

    def scan_step(state, cf, cb):
        new = []
        for d, c in ((0, cf), (1, cb)):
            st = state[d]
            st16 = st.astype(BF16)
            rows = chunk_rows(c)
            oacc[rows, :] += _dot(qp[d, rows, :], st16)
            dec = eg[d, pl.ds(pl.multiple_of(c * 8, 8), 1), :]
            new.append(st * dec - _dot(kw[d, rows, :], st16) + rr[d, rows, :])
        return tuple(new)

    n_groups = n_chunks // PREP_UNROLL
    dummy_rows = pl.ds(seq, PREP_UNROLL * CHUNK)
    oacc[...] = jnp.zeros(oacc.shape, F32)
    for d in range(2):
        qp[d, dummy_rows, :] = jnp.zeros((PREP_UNROLL * CHUNK, LANES), BF16)
        kw[d, dummy_rows, :] = jnp.zeros((PREP_UNROLL * CHUNK, LANES), BF16)
        rr[d, dummy_rows, :] = jnp.zeros((PREP_UNROLL * CHUNK, LANES), F32)
        eg[d, n_chunks * 8:, :] = jnp.zeros((PREP_UNROLL * 8, LANES), F32)

    def fwd_chunk(g, u):
        return g * PREP_UNROLL + u

    def bwd_chunk(g, u):
        return n_chunks - 1 - g * PREP_UNROLL - u

    def group_fn(i, state):
        chains = ([(fwd_chunk(i, u), 0) for u in range(PREP_UNROLL)]
                  + [(bwd_chunk(i, u), 1) for u in range(PREP_UNROLL)])
        steps = [(jnp.where(i > 0, fwd_chunk(i - 1, u), n_chunks + u),
                  jnp.where(i > 0, bwd_chunk(i - 1, u), n_chunks + u)) for u in range(PREP_UNROLL)]
        box = [state]

        def tick():
            if steps:
                cf, cb = steps.pop(0)
                box[0] = scan_step(box[0], cf, cb)

        prep_group(chains, tick)
        while steps:
            tick()
        return box[0]

    zero = jnp.zeros((DN_HEAD_DIM, DN_HEAD_DIM), F32)
    state = lax.fori_loop(0, n_groups, group_fn, (zero, zero))
    for u in range(PREP_UNROLL):
        state = scan_step(state, fwd_chunk(n_groups - 1, u), bwd_chunk(n_groups - 1, u))

    def out_fn(i, c):
        r0 = pl.multiple_of(i * OUT_TILE, OUT_TILE)
        rows = pl.ds(r0, OUT_TILE)
        o = oacc[rows, :]
        o = o * lax.rsqrt(jnp.mean(o * o, axis=-1, keepdims=True) + EPS) * nw_ref[...]
        o_ref[rows, :] = (o * _silu(z_ref[rows, :])).astype(o_ref.dtype)
        return c
    lax.fori_loop(0, seq // OUT_TILE, out_fn, 0)


def _gdn(dqkv, dz, gf, gt, dn_w, batch, seq):
    tok = lambda off: pl.BlockSpec((seq, LANES), lambda b, h: (b, h + off))
    dummy = PREP_UNROLL * CHUNK
    scratch = [
        pltpu.VMEM((2, seq + dummy, LANES), BF16),
        pltpu.VMEM((2, seq + dummy, LANES), BF16),
        pltpu.VMEM((2, seq + dummy, LANES), F32),
        pltpu.VMEM((2, (seq + dummy) // CHUNK * 8, LANES), F32),
        pltpu.VMEM((seq + dummy, LANES), F32),
    ]
    return pl.pallas_call(
        _gdn_kernel,
        grid=(batch, DN_HEADS),
        in_specs=[tok(0), tok(DN_HEADS), tok(2 * DN_HEADS),
                  pl.BlockSpec((seq, LANES), lambda b, h: (b, h)),
                  pl.BlockSpec((seq, LANES), lambda b, h: (b, 0)),
                  pl.BlockSpec((None, N_FEAT, seq), lambda b, h: (h, 0, b)),
                  pl.BlockSpec((1, LANES), lambda b, h: (0, 0))],
        out_specs=pl.BlockSpec((seq, LANES), lambda b, h: (b, h)),
        out_shape=jax.ShapeDtypeStruct((batch * seq, DN_WIDTH), BF16),
        scratch_shapes=scratch,
        compiler_params=pltpu.CompilerParams(dimension_semantics=("arbitrary", "arbitrary"),
                                             vmem_limit_bytes=VMEM_LIMIT),
        name="gdn",
    )(dqkv, dqkv, dqkv, dz, gf, gt.reshape(DN_HEADS, N_FEAT, -1), dn_w)


TM_OUT = 512


def _outproj_kernel(a_ref, d_ref, wa_ref, wd_ref, x_ref, o_ref):
    o_ref[...] = x_ref[...] + _dot(a_ref[...], wa_ref[...]) + _dot(d_ref[...], wd_ref[...])


def _outproj(attn, delta, w_a, w_d, x2d):
    t = x2d.shape[0]
    tm = TM_OUT
    row = lambda i: (i, 0)
    const = lambda i: (0, 0)
    return pl.pallas_call(
        _outproj_kernel,
        grid=(t // tm,),
        in_specs=[pl.BlockSpec((tm, NA_WIDTH), row), pl.BlockSpec((tm, DN_WIDTH), row),
                  pl.BlockSpec(w_a.shape, const), pl.BlockSpec(w_d.shape, const),
                  pl.BlockSpec((tm, D_MODEL), row)],
        out_specs=pl.BlockSpec((tm, D_MODEL), row),
        out_shape=jax.ShapeDtypeStruct((t, D_MODEL), F32),
        compiler_params=pltpu.CompilerParams(dimension_semantics=("arbitrary",),
                                             vmem_limit_bytes=VMEM_LIMIT),
        name="outproj",
    )(attn, delta, w_a, w_d, x2d)


def _gate_lane_vector(p):
    v = jnp.zeros((DN_HEADS, N_FEAT), F32)
    v = v.at[:, F_GC_F].set(p[0].astype(F32)).at[:, F_GC_B].set(p[1].astype(F32))
    return jnp.pad(v.reshape(1, N_GATE), ((0, 0), (0, LANES - N_GATE)))


def _layer(x2d, batch, seq, norm_w, w_in, gain_q, gain_k, rpb, conv_w, a_log, dt_bias, dn_norm_w, w_out):
    n_main = 4 * NA_WIDTH + 4 * DN_WIDTH
    w_main = w_in[:, :n_main].astype(BF16)
    w_gate = w_in[:, n_main:].reshape(-1, N_FEAT, DN_HEADS).transpose(0, 2, 1).reshape(-1, N_GATE)
    w_gate = jnp.pad(w_gate, ((0, 0), (0, LANES - N_GATE))).astype(BF16)
    gq = jnp.tile(gain_q.astype(F32) * NA_HEAD_DIM ** -0.5, NA_HEADS)[None, :]
    gk = jnp.tile(gain_k.astype(F32), NA_HEADS)[None, :]
    head_of = np.arange(NA_WIDTH) // NA_HEAD_DIM
    seg = jnp.asarray(head_of[:, None] == head_of[None, :], BF16)
    aq, ak, av, az, dqkv, dz, gf, gt = _inproj(
        x2d, norm_w[None, :].astype(F32), w_main, w_gate, gq, gk, seg,
        _gate_lane_vector(a_log), _gate_lane_vector(dt_bias), conv_w.astype(F32), seq)
    attn = _natten(aq, ak, av, az, _bias_table(rpb), batch, seq)
    delta = _gdn(dqkv, dz, gf, gt, dn_norm_w[None, :].astype(F32), batch, seq)
    w_o = w_out.astype(BF16)
    return _outproj(attn, delta, w_o[:NA_WIDTH], w_o[NA_WIDTH:], x2d)


def kernel(x, norm_w, w_in, qk_gain_q, qk_gain_k, rpb, conv_w, a_log, dt_bias, dn_norm_w, w_out):
    batch, seq, d = x.shape
    x2d = x.reshape(batch * seq, d)
    for l in range(norm_w.shape[0]):
        x2d = _layer(x2d, batch, seq, norm_w[l], w_in[l], qk_gain_q[l], qk_gain_k[l], rpb[l],
                     conv_w[l], a_log[l], dt_bias[l], dn_norm_w[l], w_out[l])
    return x2d.reshape(batch, seq, d)
```

```python
import functools

import jax
import jax.numpy as jnp
import numpy as np
from jax import lax
from jax.experimental import pallas as pl
from jax.experimental.pallas import tpu as pltpu

F32 = jnp.float32
BF16 = jnp.bfloat16

D_MODEL = 1024
GRID_W = 64
EPS = 1e-6
NA_HEAD_DIM = 64
NA_WIDTH = 512
NA_HEADS = 8
NA_KH = 8
NA_KW = 16
DN_HEAD_DIM = 128
DN_WIDTH = 512
DN_HEADS = 4
DN_CONV = 5
N_GATE = 4 * DN_HEADS

LANES = 128
CHUNK = 128
NEG = -1e30
LOG2E = 1.4426950408889634
VMEM_LIMIT = 56 * 1024 * 1024

N_FEAT = 4
F_BETA_F, F_GC_F, F_BETA_B, F_GC_B = 0, 1, 2, 3


def _dot(a, b):
    return jnp.dot(a, b, preferred_element_type=F32)


def _dot_nt(a, b):
    return lax.dot_general(a, b, (((1,), (1,)), ((), ())), preferred_element_type=F32)


def _dot_tn(a, b):
    return lax.dot_general(a, b, (((0,), (0,)), ((), ())), preferred_element_type=F32)


def _sigmoid(x):
    return 1.0 / (1.0 + jnp.exp(-x))


def _silu(x):
    return x * _sigmoid(x)


TM_IN = 512
HALO_IN = 16
CONV_ROWS = 128


def _inproj_kernel(tiles_per_seq, xp_ref, x_ref, xn_ref, nw_ref, w_ref, wg_ref, gq_ref, gk_ref, seg_ref,
                   alog_ref, dtb_ref, cw_ref,
                   aq_ref, ak_ref, av_ref, az_ref, dqkv_ref, dz_ref, gf_ref, gt_ref,
                   ph_ref):
    def rmsnorm(x):
        ms = jnp.mean(x * x, axis=-1, keepdims=True)
        return (x * lax.rsqrt(ms + EPS) * nw_ref[...]).astype(BF16)

    h = rmsnorm(x_ref[...])
    tm = h.shape[0]
    h_ext = jnp.concatenate([rmsnorm(xp_ref[...]), h, rmsnorm(xn_ref[...])], axis=0)

    def proj(c0, n):
        return _dot(h, w_ref[:, c0:c0 + n])

    c_dn = 4 * NA_WIDTH

    def conv_blocks(lo, hi):
        for cb in range(lo, hi):
            lanes = slice(cb * LANES, (cb + 1) * LANES)
            for r0 in range(0, tm, CONV_ROWS):
                off0 = HALO_IN - DN_CONV // 2 + r0
                y = ph_ref[off0:off0 + CONV_ROWS, lanes] * cw_ref[0:1, lanes]
                for j in range(1, DN_CONV):
                    y = y + ph_ref[off0 + j:off0 + j + CONV_ROWS, lanes] * cw_ref[j:j + 1, lanes]
                y = _silu(y)
                if cb < 2 * DN_HEADS:
                    scale = DN_HEAD_DIM ** -0.5 if cb < DN_HEADS else 1.0
                    y = y * (lax.rsqrt(jnp.sum(y * y, axis=-1, keepdims=True) + EPS) * scale)
                dqkv_ref[r0:r0 + CONV_ROWS, lanes] = y

    def head_sumsq(t):
        return _dot((t * t).astype(BF16), seg_ref[...]) * (1.0 / NA_HEAD_DIM)

    raw_tm = _dot(h, wg_ref[...])
    t_q = proj(0, NA_WIDTH)

    raw = raw_tm.T[:N_GATE, :]
    f_idx = lax.broadcasted_iota(jnp.int32, raw.shape, 0) % N_FEAT
    pos = lax.broadcasted_iota(jnp.int32, raw.shape, 1) % CHUNK
    beta = _sigmoid(raw)
    z = raw + dtb_ref[...]
    softplus = jnp.maximum(z, 0.0) + jnp.log1p(jnp.exp(-jnp.abs(z)))
    g = -jnp.exp(alog_ref[...]) * softplus
    cf = g
    cb = g
    s = 1
    while s < CHUNK:
        cf = cf + jnp.where(pos >= s, pltpu.roll(cf, s, 1), 0.0)
        cb = cb + jnp.where(pos < CHUNK - s, pltpu.roll(cb, tm - s, 1), 0.0)
        s *= 2
    feat = jnp.where(f_idx == F_GC_F, cf, jnp.where(f_idx == F_GC_B, cb, beta))
    gt_ref[...] = feat
    gf_ref[...] = jnp.concatenate([feat, jnp.zeros((LANES - N_GATE, tm), F32)], axis=0).T

    p_ext = _dot(h_ext, w_ref[:, c_dn:c_dn + 3 * DN_WIDTH])
    t_in_seq = pl.program_id(0) % tiles_per_seq
    ph_ref[0:HALO_IN, :] = jnp.where(t_in_seq > 0, p_ext[0:HALO_IN, :], 0.0)
    ph_ref[HALO_IN:HALO_IN + tm, :] = p_ext[HALO_IN:HALO_IN + tm, :]
    ph_ref[HALO_IN + tm:, :] = jnp.where(t_in_seq < tiles_per_seq - 1, p_ext[HALO_IN + tm:, :], 0.0)
    t_k = proj(NA_WIDTH, NA_WIDTH)
    m_q = head_sumsq(t_q)
    conv_blocks(0, 4)
    av_ref[...] = proj(2 * NA_WIDTH, NA_WIDTH).astype(BF16)
    m_k = head_sumsq(t_k)
    aq_ref[...] = (t_q * lax.rsqrt(m_q + EPS) * gq_ref[...]).astype(BF16)
    conv_blocks(4, 8)
    az_ref[...] = proj(3 * NA_WIDTH, NA_WIDTH)
    ak_ref[...] = (t_k * lax.rsqrt(m_k + EPS) * gk_ref[...]).astype(BF16)
    conv_blocks(8, 3 * DN_HEADS)
    dz_ref[...] = proj(c_dn + 3 * DN_WIDTH, DN_WIDTH)


def _inproj(x2d, nw, w_main, w_gate, gq, gk, seg, alog_vec, dtb_vec, conv_w, seq):
    t = x2d.shape[0]
    tm = TM_IN
    grid = (t // tm,)
    const = lambda i: (0, 0)
    row = lambda i: (i, 0)
    halo_per_tile = tm // HALO_IN
    last_halo = t // HALO_IN - 1
    prev_rows = lambda i: (jnp.maximum(i * halo_per_tile - 1, 0), 0)
    next_rows = lambda i: (jnp.minimum((i + 1) * halo_per_tile, last_halo), 0)
    out_shape = (
        jax.ShapeDtypeStruct((t, NA_WIDTH), BF16),
        jax.ShapeDtypeStruct((t, NA_WIDTH), BF16),
        jax.ShapeDtypeStruct((t, NA_WIDTH), BF16),
        jax.ShapeDtypeStruct((t, NA_WIDTH), F32),
        jax.ShapeDtypeStruct((t, 3 * DN_WIDTH), F32),
        jax.ShapeDtypeStruct((t, DN_WIDTH), F32),
        jax.ShapeDtypeStruct((t, LANES), F32),
        jax.ShapeDtypeStruct((N_GATE, t), F32),
    )
    in_specs = [
        pl.BlockSpec((HALO_IN, D_MODEL), prev_rows),
        pl.BlockSpec((tm, D_MODEL), row),
        pl.BlockSpec((HALO_IN, D_MODEL), next_rows),
        pl.BlockSpec((1, D_MODEL), const),
        pl.BlockSpec(w_main.shape, const),
        pl.BlockSpec(w_gate.shape, const),
        pl.BlockSpec((1, NA_WIDTH), const),
        pl.BlockSpec((1, NA_WIDTH), const),
        pl.BlockSpec((NA_WIDTH, NA_WIDTH), const),
        pl.BlockSpec((N_GATE, 1), const),
        pl.BlockSpec((N_GATE, 1), const),
        pl.BlockSpec(conv_w.shape, const),
    ]
    out_specs = (
        pl.BlockSpec((tm, NA_WIDTH), row),
        pl.BlockSpec((tm, NA_WIDTH), row),
        pl.BlockSpec((tm, NA_WIDTH), row),
        pl.BlockSpec((tm, NA_WIDTH), row),
        pl.BlockSpec((tm, 3 * DN_WIDTH), row),
        pl.BlockSpec((tm, DN_WIDTH), row),
        pl.BlockSpec((tm, LANES), row),
        pl.BlockSpec((N_GATE, tm), lambda i: (0, i)),
    )
    return pl.pallas_call(
        functools.partial(_inproj_kernel, seq // tm),
        grid=grid, in_specs=in_specs, out_specs=out_specs, out_shape=out_shape,
        scratch_shapes=[pltpu.VMEM((tm + 2 * HALO_IN, 3 * DN_WIDTH), F32)],
        compiler_params=pltpu.CompilerParams(dimension_semantics=("arbitrary",),
                                             vmem_limit_bytes=VMEM_LIMIT),
        name="inproj",
    )(x2d, x2d, x2d, nw, w_main, w_gate, gq, gk, seg, alog_vec, dtb_vec, conv_w)


NA_GROUP = 4
NA_GW = NA_GROUP * NA_HEAD_DIM
NA_KEYS = NA_KH * GRID_W
N_DR = 2 * NA_KH - 2
NA_ROWS = 4


def _natten_kernel(q_ref, k_ref, v_ref, z_ref, bias_ref, o_ref):
    rows = q_ref.shape[0] // GRID_W
    lane_head = lax.broadcasted_iota(jnp.int32, (GRID_W, NA_GW), 1) // NA_HEAD_DIM

    def rows_fn(i, carry):
        rr = [i * NA_ROWS + u for u in range(NA_ROWS)]
        rs = [jnp.clip(r - NA_KH // 2, 0, rows - NA_KH) for r in rr]
        q0 = [pl.multiple_of(r * GRID_W, GRID_W) for r in rr]
        k0 = [pl.multiple_of(x * GRID_W, GRID_W) for x in rs]
        s = []
        for u in range(NA_ROWS):
            q = q_ref[pl.ds(q0[u], GRID_W), :]
            zero = jnp.zeros_like(q)
            q4 = jnp.concatenate([jnp.where(lane_head == hh, q, zero) for hh in range(NA_GROUP)], axis=0)
            s.append(_dot_nt(q4, k_ref[pl.ds(k0[u], NA_KEYS), :]))
        p, linv = [], []
        for u in range(NA_ROWS):
            dr0 = rs[u] - rr[u] + (NA_KH - 1)
            bias = jnp.concatenate(
                [jnp.concatenate([bias_ref[hh, dr0 + 2 * j] for j in range(NA_KH // 2)], axis=1)
                 for hh in range(NA_GROUP)], axis=0)
            sb = s[u] + bias
            e = jnp.exp2(sb - jnp.max(sb, axis=-1, keepdims=True))
            linv.append(1.0 / jnp.sum(e, axis=-1, keepdims=True))
            p.append(e.astype(BF16))
        o = [_dot(p[u], v_ref[pl.ds(k0[u], NA_KEYS), :]) for u in range(NA_ROWS)]
        for u in range(NA_ROWS):
            on = o[u] * linv[u]
            out = jnp.zeros((GRID_W, NA_GW), F32)
            for hh in range(NA_GROUP):
                out = jnp.where(lane_head == hh, on[hh * GRID_W:(hh + 1) * GRID_W, :], out)
            z = z_ref[pl.ds(q0[u], GRID_W), :]
            o_ref[pl.ds(q0[u], GRID_W), :] = (out * _silu(z)).astype(o_ref.dtype)
        return carry

    lax.fori_loop(0, rows // NA_ROWS, rows_fn, 0)


def _natten(aq, ak, av, az, bias_tab, batch, seq):
    n_groups = NA_HEADS // NA_GROUP
    blk = lambda b, g: (b, g)
    spec = pl.BlockSpec((seq, NA_GW), blk)
    return pl.pallas_call(
        _natten_kernel,
        grid=(batch, n_groups),
        in_specs=[spec, spec, spec, spec,
                  pl.BlockSpec((NA_GROUP, N_DR, GRID_W, LANES), lambda b, g: (g, 0, 0, 0))],
        out_specs=spec,
        out_shape=jax.ShapeDtypeStruct((batch * seq, NA_WIDTH), BF16),
        compiler_params=pltpu.CompilerParams(dimension_semantics=("arbitrary", "arbitrary"),
                                             vmem_limit_bytes=VMEM_LIMIT),
        name="natten",
    )(aq, ak, av, az, bias_tab)


def _bias_table(rpb):
    cols = np.arange(GRID_W)
    cs = np.clip(cols - NA_KW // 2, 0, GRID_W - NA_KW)
    kc = cols[None, :]
    c = cols[:, None]
    valid = (kc >= cs[:, None]) & (kc < cs[:, None] + NA_KW)
    idx = np.clip(kc - c + NA_KW - 1, 0, 2 * NA_KW - 2)
    tab = jnp.where(valid[None, None], rpb[:, :, idx] * LOG2E, NEG)
    return jnp.concatenate([tab[:, :-1], tab[:, 1:]], axis=-1).astype(F32)


OUT_TILE = 256
HALF = CHUNK // 2
N_SQUARINGS = 4
PREP_UNROLL = 8


def _triangular_inverse(ms, directions, tick):
    lo = lax.broadcasted_iota(jnp.int32, (HALF, CHUNK), 1) < HALF

    def bf(x):
        return x.astype(BF16)

    def block_diag(x16):
        z = jnp.zeros_like(x16)
        return jnp.concatenate([jnp.where(lo, x16, z), jnp.where(lo, z, x16)], axis=0)

    def below(x16):
        return jnp.concatenate([jnp.zeros_like(x16), x16], axis=0)

    def above(x16):
        return jnp.concatenate([x16, jnp.zeros_like(x16)], axis=0)

    top = [m[:HALF, :] for m in ms]
    bot = [m[HALF:, :] for m in ms]
    pk = [jnp.where(lo, t, b) for t, b in zip(top, bot)]
    m16 = [bf(p) for p in pk]
    e = list(pk)
    pw16 = m16
    for _ in range(N_SQUARINGS):
        pw = [_dot(p, block_diag(p)) for p in pw16]
        tick()
        pw16 = [bf(p) for p in pw]
        ep = [_dot(bf(x), block_diag(p)) for x, p in zip(e, pw16)]
        tick()
        e = [x + p + y for x, p, y in zip(e, pw, ep)]
    res = [m - x + _dot(a, block_diag(bf(x))) for m, a, x in zip(pk, m16, e)]
    tick()
    er = [_dot(bf(x), block_diag(bf(r))) for x, r in zip(e, res)]
    tick()
    e = [x + r + y for x, r, y in zip(e, res, er)]
    off, t = [], []
    for d, tp, bt, x in zip(directions, top, bot, e):
        if d == 0:
            o = jnp.where(lo, bt, 0.0)
            t.append(o + _dot(bf(jnp.where(lo, 0.0, x)), below(bf(o))))
        else:
            o = jnp.where(lo, 0.0, tp)
            t.append(o + _dot(bf(jnp.where(lo, x, 0.0)), above(bf(o))))
        off.append(o)
    tick()
    out = []
    for d, tt, x in zip(directions, t, e):
        if d == 0:
            eoff = tt + _dot(bf(tt), above(bf(jnp.where(lo, x, 0.0))))
            out.append(jnp.concatenate([jnp.where(lo, x, 0.0), jnp.where(lo, eoff, x)], axis=0))
        else:
            eoff = tt + _dot(bf(tt), below(bf(jnp.where(lo, 0.0, x))))
            out.append(jnp.concatenate([jnp.where(lo, x, eoff), jnp.where(lo, 0.0, x)], axis=0))
    tick()
    return out


def _gdn_kernel(qn, kn, vn, gf_ref, gt_ref,
                o_ref,
                qp, kw, rr, eg):
    seq = qn.shape[0]
    n_chunks = seq // CHUNK
    h = pl.program_id(1)

    ii = lax.broadcasted_iota(jnp.int32, (CHUNK, CHUNK), 0)
    jj = lax.broadcasted_iota(jnp.int32, (CHUNK, CHUNK), 1)
    shift = (LANES - h * N_FEAT) % LANES

    dirs = ((F_BETA_F, F_GC_F), (F_BETA_B, F_GC_B))

    def chunk_rows(c):
        return pl.ds(pl.multiple_of(c * CHUNK, CHUNK), CHUNK)

    def prep_group(chains, tick):
        rows_of = [chunk_rows(c) for c, _ in chains]
        k32 = [kn[r, :] for r in rows_of]
        k16 = [k.astype(BF16) for k in k32]
        kk0 = [_dot_nt(k, k) for k in k16]
        tick()
        bb, bg, glast, decay, m = [], [], [], [], []
        for u, (c, d) in enumerate(chains):
            fb, fg = dirs[d]
            gfr = pltpu.roll(gf_ref[rows_of[u], :], shift, 1)
            bb.append(jnp.broadcast_to(gfr[:, fb:fb + 1], (CHUNK, LANES)))
            g = jnp.broadcast_to(gfr[:, fg:fg + 1], (CHUNK, LANES))
            bg.append(g)
            glast.append(g[CHUNK - 1:CHUNK, :] if d == 0 else g[0:1, :])
            gcr = gt_ref[fg:fg + 1, rows_of[u]]
            incl = (jj <= ii) if d == 0 else (jj >= ii)
            strict = (jj < ii) if d == 0 else (jj > ii)
            decay.append(jnp.exp(jnp.where(incl, g - gcr, NEG)))
            m.append(jnp.where(strict, -(bb[-1] * kk0[u] * decay[-1]), 0.0))
        e = _triangular_inverse(m, [d for _, d in chains], tick)
        sols = []
        for u, (x, b, g) in enumerate(zip(e, bb, bg)):
            rhs = jnp.concatenate([vn[rows_of[u], :] * b, k32[u] * (b * jnp.exp(g))], axis=1)
            sols.append((rhs + _dot(x.astype(BF16), rhs.astype(BF16))).astype(BF16))
        tick()
        q32 = [qn[r, :] for r in rows_of]
        qk0 = [_dot_nt(q.astype(BF16), k) for q, k in zip(q32, k16)]
        tick()
        qos, trs = [], []
        for u, (sol16, g, gl, dc) in enumerate(zip(sols, bg, glast, decay)):
            qos.append(_dot((qk0[u] * dc).astype(BF16), sol16))
            kdec = (k32[u] * jnp.exp(gl - g)).astype(BF16)
            trs.append(_dot_tn(kdec, sol16))
        tick()
        for (c, d), rows, q, qo, tr, g, gl in zip(chains, rows_of, q32, qos, trs, bg, glast):
            qp[d, rows, :] = (q * jnp.exp(g) - qo[:, DN_HEAD_DIM:]).astype(BF16)
            kw[d, rows, :] = tr[:, DN_HEAD_DIM:].astype(BF16)
            rr[d, rows, :] = tr[:, :DN_HEAD_DIM]
            eg[d, pl.ds(pl.multiple_of(c * 8, 8), 8), :] = jnp.broadcast_to(jnp.exp(gl), (8, LANES))
            o_ref[rows, :] += qo[:, :DN_HEAD_DIM]

    def scan_step(state, cf, cb, out_f, out_b):
        new = []
        for d, c, c_out in ((0, cf, out_f), (1, cb, out_b)):
            st = state[d]
            st16 = st.astype(BF16)
            rows = chunk_rows(c)
            o_ref[chunk_rows(c_out), :] += _dot(qp[d, rows, :], st16)
            dec = eg[d, pl.ds(pl.multiple_of(c * 8, 8), 1), :]
            new.append(st * dec - _dot(kw[d, rows, :], st16) + rr[d, rows, :])
        return tuple(new)

    n_groups = n_chunks // PREP_UNROLL
    dummy_rows = pl.ds(seq, PREP_UNROLL * CHUNK)
    o_ref[...] = jnp.zeros(o_ref.shape, F32)
    for d in range(2):
        qp[d, dummy_rows, :] = jnp.zeros((PREP_UNROLL * CHUNK, LANES), BF16)
        kw[d, dummy_rows, :] = jnp.zeros((PREP_UNROLL * CHUNK, LANES), BF16)
        rr[d, dummy_rows, :] = jnp.zeros((PREP_UNROLL * CHUNK, LANES), F32)
        eg[d, n_chunks * 8:, :] = jnp.zeros((PREP_UNROLL * 8, LANES), F32)

    def fwd_chunk(g, u):
        return g * PREP_UNROLL + u

    def bwd_chunk(g, u):
        return n_chunks - 1 - g * PREP_UNROLL - u

    def group_fn(i, state):
        chains = ([(fwd_chunk(i, u), 0) for u in range(PREP_UNROLL)]
                  + [(bwd_chunk(i, u), 1) for u in range(PREP_UNROLL)])
        steps = []
        for u in range(PREP_UNROLL):
            cf, cb = fwd_chunk(i - 1, u), bwd_chunk(i - 1, u)
            steps.append((jnp.where(i > 0, cf, n_chunks + u), jnp.where(i > 0, cb, n_chunks + u),
                          jnp.where(i > 0, cf, 0), jnp.where(i > 0, cb, 0)))
        box = [state]

        def tick():
            if steps:
                box[0] = scan_step(box[0], *steps.pop(0))

        prep_group(chains, tick)
        while steps:
            tick()
        return box[0]

    zero = jnp.zeros((DN_HEAD_DIM, DN_HEAD_DIM), F32)
    state = lax.fori_loop(0, n_groups, group_fn, (zero, zero))
    for u in range(PREP_UNROLL):
        cf, cb = fwd_chunk(n_groups - 1, u), bwd_chunk(n_groups - 1, u)
        state = scan_step(state, cf, cb, cf, cb)


def _gdn(dqkv, gf, gt, batch, seq):
    tok = lambda off: pl.BlockSpec((seq, LANES), lambda b, h: (b, h + off))
    dummy = PREP_UNROLL * CHUNK
    scratch = [
        pltpu.VMEM((2, seq + dummy, LANES), BF16),
        pltpu.VMEM((2, seq + dummy, LANES), BF16),
        pltpu.VMEM((2, seq + dummy, LANES), F32),
        pltpu.VMEM((2, (seq + dummy) // CHUNK * 8, LANES), F32),
    ]
    return pl.pallas_call(
        _gdn_kernel,
        grid=(batch, DN_HEADS),
        in_specs=[tok(0), tok(DN_HEADS), tok(2 * DN_HEADS),
                  pl.BlockSpec((seq, LANES), lambda b, h: (b, 0)),
                  pl.BlockSpec((None, N_FEAT, seq), lambda b, h: (h, 0, b))],
        out_specs=pl.BlockSpec((seq, LANES), lambda b, h: (b, h)),
        out_shape=jax.ShapeDtypeStruct((batch * seq, DN_WIDTH), F32),
        scratch_shapes=scratch,
        compiler_params=pltpu.CompilerParams(dimension_semantics=("arbitrary", "arbitrary"),
                                             vmem_limit_bytes=VMEM_LIMIT),
        name="gdn",
    )(dqkv, dqkv, dqkv, gf, gt.reshape(DN_HEADS, N_FEAT, -1))


TM_OUT = 512


def _outproj_kernel(a_ref, d_ref, z_ref, nw_ref, wa_ref, wd_ref, x_ref, o_ref):
    heads = []
    for hh in range(DN_HEADS):
        lanes = slice(hh * DN_HEAD_DIM, (hh + 1) * DN_HEAD_DIM)
        o = d_ref[:, lanes]
        o = o * lax.rsqrt(jnp.mean(o * o, axis=-1, keepdims=True) + EPS) * nw_ref[...]
        heads.append((o * _silu(z_ref[:, lanes])).astype(BF16))
    delta = jnp.concatenate(heads, axis=1)
    o_ref[...] = x_ref[...] + _dot(a_ref[...], wa_ref[...]) + _dot(delta, wd_ref[...])


def _outproj(attn, dn_out, dz, dn_w, w_a, w_d, x2d):
    t = x2d.shape[0]
    tm = TM_OUT
    row = lambda i: (i, 0)
    const = lambda i: (0, 0)
    return pl.pallas_call(
        _outproj_kernel,
        grid=(t // tm,),
        in_specs=[pl.BlockSpec((tm, NA_WIDTH), row), pl.BlockSpec((tm, DN_WIDTH), row),
                  pl.BlockSpec((tm, DN_WIDTH), row), pl.BlockSpec((1, DN_HEAD_DIM), const),
                  pl.BlockSpec(w_a.shape, const), pl.BlockSpec(w_d.shape, const),
                  pl.BlockSpec((tm, D_MODEL), row)],
        out_specs=pl.BlockSpec((tm, D_MODEL), row),
        out_shape=jax.ShapeDtypeStruct((t, D_MODEL), F32),
        compiler_params=pltpu.CompilerParams(dimension_semantics=("arbitrary",),
                                             vmem_limit_bytes=VMEM_LIMIT),
        name="outproj",
    )(attn, dn_out, dz, dn_w, w_a, w_d, x2d)


def _gate_column(p):
    v = jnp.zeros((DN_HEADS, N_FEAT), F32)
    v = v.at[:, F_GC_F].set(p[0].astype(F32)).at[:, F_GC_B].set(p[1].astype(F32))
    return v.reshape(N_GATE, 1)


def _layer(x2d, batch, seq, norm_w, w_in, gain_q, gain_k, rpb, conv_w, a_log, dt_bias, dn_norm_w, w_out):
    n_main = 4 * NA_WIDTH + 4 * DN_WIDTH
    w_main = w_in[:, :n_main].astype(BF16)
    w_gate = w_in[:, n_main:].reshape(-1, N_FEAT, DN_HEADS).transpose(0, 2, 1).reshape(-1, N_GATE)
    w_gate = jnp.pad(w_gate, ((0, 0), (0, LANES - N_GATE))).astype(BF16)
    gq = jnp.tile(gain_q.astype(F32) * (NA_HEAD_DIM ** -0.5 * LOG2E), NA_HEADS)[None, :]
    gk = jnp.tile(gain_k.astype(F32), NA_HEADS)[None, :]
    head_of = np.arange(NA_WIDTH) // NA_HEAD_DIM
    seg = jnp.asarray(head_of[:, None] == head_of[None, :], BF16)
    aq, ak, av, az, dqkv, dz, gf, gt = _inproj(
        x2d, norm_w[None, :].astype(F32), w_main, w_gate, gq, gk, seg,
        _gate_column(a_log), _gate_column(dt_bias), conv_w.astype(F32), seq)
    attn = _natten(aq, ak, av, az, _bias_table(rpb), batch, seq)
    dn_out = _gdn(dqkv, gf, gt, batch, seq)
    w_o = w_out.astype(BF16)
    return _outproj(attn, dn_out, dz, dn_norm_w[None, :].astype(F32), w_o[:NA_WIDTH], w_o[NA_WIDTH:], x2d)


def kernel(x, norm_w, w_in, qk_gain_q, qk_gain_k, rpb, conv_w, a_log, dt_bias, dn_norm_w, w_out):
    batch, seq, d = x.shape
    x2d = x.reshape(batch * seq, d)
    for l in range(norm_w.shape[0]):
        x2d = _layer(x2d, batch, seq, norm_w[l], w_in[l], qk_gain_q[l], qk_gain_k[l], rpb[l],
                     conv_w[l], a_log[l], dt_bias[l], dn_norm_w[l], w_out[l])
    return x2d.reshape(batch, seq, d)
```

```python
import functools

import jax
import jax.numpy as jnp
import numpy as np
from jax import lax
from jax.experimental import pallas as pl
from jax.experimental.pallas import tpu as pltpu

F32 = jnp.float32
BF16 = jnp.bfloat16

D_MODEL = 1024
GRID_W = 64
EPS = 1e-6
NA_HEAD_DIM = 64
NA_WIDTH = 512
NA_HEADS = 8
NA_KH = 8
NA_KW = 16
DN_HEAD_DIM = 128
DN_WIDTH = 512
DN_HEADS = 4
DN_CONV = 5
N_GATE = 4 * DN_HEADS

LANES = 128
CHUNK = 128
NEG = -1e30
LOG2E = 1.4426950408889634
VMEM_LIMIT = 56 * 1024 * 1024

N_FEAT = 4
F_BETA_F, F_GC_F, F_BETA_B, F_GC_B = 0, 1, 2, 3


def _dot(a, b):
    return jnp.dot(a, b, preferred_element_type=F32)


def _dot_nt(a, b):
    return lax.dot_general(a, b, (((1,), (1,)), ((), ())), preferred_element_type=F32)


def _dot_tn(a, b):
    return lax.dot_general(a, b, (((0,), (0,)), ((), ())), preferred_element_type=F32)


def _sigmoid(x):
    return 1.0 / (1.0 + jnp.exp(-x))


def _silu(x):
    return x * _sigmoid(x)


TM_IN = 512
HALO_IN = 16
CONV_ROWS = 256
SUBLANES = 8


def _inproj_kernel(tiles_per_seq, xp_ref, x_ref, xn_ref, nw_ref, w_ref, wg_ref, gq_ref, gk_ref, seg_ref,
                   alog_ref, dtb_ref, cw_ref,
                   aq_ref, ak_ref, av_ref, az_ref, dqkv_ref, dz_ref, gf_ref, gt_ref,
                   ph_ref):
    def rmsnorm(x):
        ms = jnp.mean(x * x, axis=-1, keepdims=True)
        return (x * lax.rsqrt(ms + EPS) * nw_ref[...]).astype(BF16)

    h = rmsnorm(x_ref[...])
    tm = h.shape[0]
    h_ext = jnp.concatenate([rmsnorm(xp_ref[...]), h, rmsnorm(xn_ref[...])], axis=0)

    def proj(c0, n):
        return _dot(h, w_ref[:, c0:c0 + n])

    c_dn = 4 * NA_WIDTH

    def conv_blocks(lo, hi):
        for cb in range(lo, hi):
            lanes = slice(cb * LANES, (cb + 1) * LANES)
            for r0 in range(0, tm, CONV_ROWS):
                n_win = CONV_ROWS + 2 * SUBLANES
                win = ph_ref[r0 + HALO_IN - SUBLANES:r0 + HALO_IN + CONV_ROWS + SUBLANES, lanes]
                mid = DN_CONV // 2
                y = win[SUBLANES:SUBLANES + CONV_ROWS, :] * cw_ref[mid:mid + 1, lanes]
                for j in range(DN_CONV):
                    if j != mid:
                        rolled = pltpu.roll(win, (mid - j) % n_win, 0)
                        y = y + rolled[SUBLANES:SUBLANES + CONV_ROWS, :] * cw_ref[j:j + 1, lanes]
                y = _silu(y)
                if cb < 2 * DN_HEADS:
                    scale = DN_HEAD_DIM ** -0.5 if cb < DN_HEADS else 1.0
                    y = y * (lax.rsqrt(jnp.sum(y * y, axis=-1, keepdims=True) + EPS) * scale)
                dqkv_ref[r0:r0 + CONV_ROWS, lanes] = y

    def head_sumsq(t):
        return _dot((t * t).astype(BF16), seg_ref[...])

    raw_tm = _dot(h, wg_ref[...])
    t_q = proj(0, NA_WIDTH)

    raw = raw_tm.T[:N_GATE, :]
    f_idx = lax.broadcasted_iota(jnp.int32, raw.shape, 0) % N_FEAT
    pos = lax.broadcasted_iota(jnp.int32, raw.shape, 1) % CHUNK
    beta = _sigmoid(raw)
    z = raw + dtb_ref[...]
    softplus = jnp.maximum(z, 0.0) + jnp.log1p(jnp.exp(-jnp.abs(z)))
    g = -jnp.exp(alog_ref[...]) * softplus
    cf = g
    cb = g
    s = 1
    while s < CHUNK:
        cf = cf + jnp.where(pos >= s, pltpu.roll(cf, s, 1), 0.0)
        cb = cb + jnp.where(pos < CHUNK - s, pltpu.roll(cb, tm - s, 1), 0.0)
        s *= 2
    feat = jnp.where(f_idx == F_GC_F, cf, jnp.where(f_idx == F_GC_B, cb, beta))
    gt_ref[...] = feat
    gf_ref[...] = jnp.concatenate([feat, jnp.zeros((LANES - N_GATE, tm), F32)], axis=0).T

    p_ext = _dot(h_ext, w_ref[:, c_dn:c_dn + 3 * DN_WIDTH])
    t_in_seq = pl.program_id(0) % tiles_per_seq
    ph_ref[0:HALO_IN, :] = jnp.where(t_in_seq > 0, p_ext[0:HALO_IN, :], 0.0)
    ph_ref[HALO_IN:HALO_IN + tm, :] = p_ext[HALO_IN:HALO_IN + tm, :]
    ph_ref[HALO_IN + tm:, :] = jnp.where(t_in_seq < tiles_per_seq - 1, p_ext[HALO_IN + tm:, :], 0.0)
    t_k = proj(NA_WIDTH, NA_WIDTH)
    m_q = head_sumsq(t_q)
    conv_blocks(0, 4)
    av_ref[...] = proj(2 * NA_WIDTH, NA_WIDTH).astype(BF16)
    m_k = head_sumsq(t_k)
    aq_ref[...] = (t_q * lax.rsqrt(m_q + EPS) * gq_ref[...]).astype(BF16)
    conv_blocks(4, 8)
    az_ref[...] = proj(3 * NA_WIDTH, NA_WIDTH)
    ak_ref[...] = (t_k * lax.rsqrt(m_k + EPS) * gk_ref[...]).astype(BF16)
    conv_blocks(8, 3 * DN_HEADS)
    dz_ref[...] = proj(c_dn + 3 * DN_WIDTH, DN_WIDTH)


def _inproj(x2d, nw, w_main, w_gate, gq, gk, seg, alog_vec, dtb_vec, conv_w, seq):
    t = x2d.shape[0]
    tm = TM_IN
    grid = (t // tm,)
    const = lambda i: (0, 0)
    row = lambda i: (i, 0)
    halo_per_tile = tm // HALO_IN
    last_halo = t // HALO_IN - 1
    prev_rows = lambda i: (jnp.maximum(i * halo_per_tile - 1, 0), 0)
    next_rows = lambda i: (jnp.minimum((i + 1) * halo_per_tile, last_halo), 0)
    out_shape = (
        jax.ShapeDtypeStruct((t, NA_WIDTH), BF16),
        jax.ShapeDtypeStruct((t, NA_WIDTH), BF16),
        jax.ShapeDtypeStruct((t, NA_WIDTH), BF16),
        jax.ShapeDtypeStruct((t, NA_WIDTH), F32),
        jax.ShapeDtypeStruct((t, 3 * DN_WIDTH), F32),
        jax.ShapeDtypeStruct((t, DN_WIDTH), F32),
        jax.ShapeDtypeStruct((t, LANES), F32),
        jax.ShapeDtypeStruct((N_GATE, t), F32),
    )
    in_specs = [
        pl.BlockSpec((HALO_IN, D_MODEL), prev_rows),
        pl.BlockSpec((tm, D_MODEL), row),
        pl.BlockSpec((HALO_IN, D_MODEL), next_rows),
        pl.BlockSpec((1, D_MODEL), const),
        pl.BlockSpec(w_main.shape, const),
        pl.BlockSpec(w_gate.shape, const),
        pl.BlockSpec((1, NA_WIDTH), const),
        pl.BlockSpec((1, NA_WIDTH), const),
        pl.BlockSpec((NA_WIDTH, NA_WIDTH), const),
        pl.BlockSpec((N_GATE, 1), const),
        pl.BlockSpec((N_GATE, 1), const),
        pl.BlockSpec(conv_w.shape, const),
    ]
    out_specs = (
        pl.BlockSpec((tm, NA_WIDTH), row),
        pl.BlockSpec((tm, NA_WIDTH), row),
        pl.BlockSpec((tm, NA_WIDTH), row),
        pl.BlockSpec((tm, NA_WIDTH), row),
        pl.BlockSpec((tm, 3 * DN_WIDTH), row),
        pl.BlockSpec((tm, DN_WIDTH), row),
        pl.BlockSpec((tm, LANES), row),
        pl.BlockSpec((N_GATE, tm), lambda i: (0, i)),
    )
    return pl.pallas_call(
        functools.partial(_inproj_kernel, seq // tm),
        grid=grid, in_specs=in_specs, out_specs=out_specs, out_shape=out_shape,
        scratch_shapes=[pltpu.VMEM((tm + 2 * HALO_IN, 3 * DN_WIDTH), F32)],
        compiler_params=pltpu.CompilerParams(dimension_semantics=("arbitrary",),
                                             vmem_limit_bytes=VMEM_LIMIT),
        name="inproj",
    )(x2d, x2d, x2d, nw, w_main, w_gate, gq, gk, seg, alog_vec, dtb_vec, conv_w)


NA_GROUP = 4
NA_GW = NA_GROUP * NA_HEAD_DIM
NA_KEYS = NA_KH * GRID_W
N_DR = 2 * NA_KH - 2
NA_ROWS = 8


def _natten_kernel(q_ref, k_ref, v_ref, z_ref, bias_ref, o_ref):
    rows = q_ref.shape[0] // GRID_W
    lane_head = lax.broadcasted_iota(jnp.int32, (GRID_W, NA_GW), 1) // NA_HEAD_DIM

    def rows_fn(i, carry):
        rr = [i * NA_ROWS + u for u in range(NA_ROWS)]
        rs = [jnp.clip(r - NA_KH // 2, 0, rows - NA_KH) for r in rr]
        q0 = [pl.multiple_of(r * GRID_W, GRID_W) for r in rr]
        k0 = [pl.multiple_of(x * GRID_W, GRID_W) for x in rs]
        s = []
        for u in range(NA_ROWS):
            q = q_ref[pl.ds(q0[u], GRID_W), :]
            zero = jnp.zeros_like(q)
            q4 = jnp.concatenate([jnp.where(lane_head == hh, q, zero) for hh in range(NA_GROUP)], axis=0)
            s.append(_dot_nt(q4, k_ref[pl.ds(k0[u], NA_KEYS), :]))
        p, linv = [], []
        for u in range(NA_ROWS):
            dr0 = rs[u] - rr[u] + (NA_KH - 1)
            bias = jnp.concatenate(
                [jnp.concatenate([bias_ref[hh, dr0 + 2 * j] for j in range(NA_KH // 2)], axis=1)
                 for hh in range(NA_GROUP)], axis=0)
            sb = s[u] + bias
            e = jnp.exp2(sb - jnp.max(sb, axis=-1, keepdims=True))
            linv.append(1.0 / jnp.sum(e, axis=-1, keepdims=True))
            p.append(e.astype(BF16))
        o = [_dot(p[u], v_ref[pl.ds(k0[u], NA_KEYS), :]) for u in range(NA_ROWS)]
        for u in range(NA_ROWS):
            on = o[u] * linv[u]
            out = jnp.zeros((GRID_W, NA_GW), F32)
            for hh in range(NA_GROUP):
                out = jnp.where(lane_head == hh, on[hh * GRID_W:(hh + 1) * GRID_W, :], out)
            z = z_ref[pl.ds(q0[u], GRID_W), :]
            o_ref[pl.ds(q0[u], GRID_W), :] = (out * _silu(z)).astype(o_ref.dtype)
        return carry

    lax.fori_loop(0, rows // NA_ROWS, rows_fn, 0)


def _natten(aq, ak, av, az, bias_tab, batch, seq):
    n_groups = NA_HEADS // NA_GROUP
    blk = lambda b, g: (b, g)
    spec = pl.BlockSpec((seq, NA_GW), blk)
    return pl.pallas_call(
        _natten_kernel,
        grid=(batch, n_groups),
        in_specs=[spec, spec, spec, spec,
                  pl.BlockSpec((NA_GROUP, N_DR, GRID_W, LANES), lambda b, g: (g, 0, 0, 0))],
        out_specs=spec,
        out_shape=jax.ShapeDtypeStruct((batch * seq, NA_WIDTH), BF16),
        compiler_params=pltpu.CompilerParams(dimension_semantics=("arbitrary", "arbitrary"),
                                             vmem_limit_bytes=VMEM_LIMIT),
        name="natten",
    )(aq, ak, av, az, bias_tab)


def _bias_table(rpb):
    cols = np.arange(GRID_W)
    cs = np.clip(cols - NA_KW // 2, 0, GRID_W - NA_KW)
    kc = cols[None, :]
    c = cols[:, None]
    valid = (kc >= cs[:, None]) & (kc < cs[:, None] + NA_KW)
    onehot = (kc - c + NA_KW - 1)[None] == np.arange(2 * NA_KW - 1)[:, None, None]
    tab = jnp.einsum('hdk,kcn->hdcn', rpb.astype(F32), jnp.asarray(onehot, F32),
                     precision=lax.Precision.HIGHEST)
    tab = jnp.where(valid[None, None], tab * LOG2E, NEG)
    return jnp.concatenate([tab[:, :-1], tab[:, 1:]], axis=-1).astype(F32)


OUT_TILE = 256
HALF = CHUNK // 2
N_SQUARINGS = 4
PREP_UNROLL = 8


def _triangular_inverse(ms, directions, tick):
    lo = lax.broadcasted_iota(jnp.int32, (HALF, CHUNK), 1) < HALF

    def bf(x):
        return x.astype(BF16)

    def block_diag(x16):
        z = jnp.zeros_like(x16)
        return jnp.concatenate([jnp.where(lo, x16, z), jnp.where(lo, z, x16)], axis=0)

    def below(x16):
        return jnp.concatenate([jnp.zeros_like(x16), x16], axis=0)

    def above(x16):
        return jnp.concatenate([x16, jnp.zeros_like(x16)], axis=0)

    top = [m[:HALF, :] for m in ms]
    bot = [m[HALF:, :] for m in ms]
    pk = [jnp.where(lo, t, b) for t, b in zip(top, bot)]
    m16 = [bf(p) for p in pk]
    e = list(pk)
    pw16 = m16
    for _ in range(N_SQUARINGS):
        pw = [_dot(p, block_diag(p)) for p in pw16]
        tick()
        pw16 = [bf(p) for p in pw]
        ep = [_dot(bf(x), block_diag(p)) for x, p in zip(e, pw16)]
        tick()
        e = [x + p + y for x, p, y in zip(e, pw, ep)]
    res = [m - x + _dot(a, block_diag(bf(x))) for m, a, x in zip(pk, m16, e)]
    tick()
    er = [_dot(bf(x), block_diag(bf(r))) for x, r in zip(e, res)]
    tick()
    e = [x + r + y for x, r, y in zip(e, res, er)]
    off, t = [], []
    for d, tp, bt, x in zip(directions, top, bot, e):
        if d == 0:
            o = jnp.where(lo, bt, 0.0)
            t.append(o + _dot(bf(jnp.where(lo, 0.0, x)), below(bf(o))))
        else:
            o = jnp.where(lo, 0.0, tp)
            t.append(o + _dot(bf(jnp.where(lo, x, 0.0)), above(bf(o))))
        off.append(o)
    tick()
    out = []
    for d, tt, x in zip(directions, t, e):
        if d == 0:
            eoff = tt + _dot(bf(tt), above(bf(jnp.where(lo, x, 0.0))))
            out.append(jnp.concatenate([jnp.where(lo, x, 0.0), jnp.where(lo, eoff, x)], axis=0))
        else:
            eoff = tt + _dot(bf(tt), below(bf(jnp.where(lo, 0.0, x))))
            out.append(jnp.concatenate([jnp.where(lo, x, eoff), jnp.where(lo, 0.0, x)], axis=0))
    tick()
    return out


def _gdn_kernel(qn, kn, vn, gf_ref, gt_ref,
                o_ref,
                qp, kw, rr, eg):
    seq = qn.shape[0]
    n_chunks = seq // CHUNK
    h = pl.program_id(1)

    ii = lax.broadcasted_iota(jnp.int32, (CHUNK, CHUNK), 0)
    jj = lax.broadcasted_iota(jnp.int32, (CHUNK, CHUNK), 1)
    shift = (LANES - h * N_FEAT) % LANES

    dirs = ((F_BETA_F, F_GC_F), (F_BETA_B, F_GC_B))

    def chunk_rows(c):
        return pl.ds(pl.multiple_of(c * CHUNK, CHUNK), CHUNK)

    def prep_group(chains, tick):
        rows_of = [chunk_rows(c) for c, _ in chains]
        k32 = [kn[r, :] for r in rows_of]
        k16 = [k.astype(BF16) for k in k32]
        kk0 = [_dot_nt(k, k) for k in k16]
        tick()
        bb, bg, glast, decay, m = [], [], [], [], []
        for u, (c, d) in enumerate(chains):
            fb, fg = dirs[d]
            gfr = pltpu.roll(gf_ref[rows_of[u], :], shift, 1)
            bb.append(jnp.broadcast_to(gfr[:, fb:fb + 1], (CHUNK, LANES)))
            g = jnp.broadcast_to(gfr[:, fg:fg + 1], (CHUNK, LANES))
            bg.append(g)
            glast.append(g[CHUNK - 1:CHUNK, :] if d == 0 else g[0:1, :])
            gcr = gt_ref[fg:fg + 1, rows_of[u]]
            incl = (jj <= ii) if d == 0 else (jj >= ii)
            strict = (jj < ii) if d == 0 else (jj > ii)
            decay.append(jnp.exp(jnp.where(incl, g - gcr, NEG)))
            m.append(jnp.where(strict, -(bb[-1] * kk0[u] * decay[-1]), 0.0))
        e = _triangular_inverse(m, [d for _, d in chains], tick)
        sols = []
        for u, (x, b, g) in enumerate(zip(e, bb, bg)):
            rhs = jnp.concatenate([vn[rows_of[u], :] * b, k32[u] * (b * jnp.exp(g))], axis=1)
            sols.append((rhs + _dot(x.astype(BF16), rhs.astype(BF16))).astype(BF16))
        tick()
        q32 = [qn[r, :] for r in rows_of]
        qk0 = [_dot_nt(q.astype(BF16), k) for q, k in zip(q32, k16)]
        tick()
        qos, trs = [], []
        for u, (sol16, g, gl, dc) in enumerate(zip(sols, bg, glast, decay)):
            qos.append(_dot((qk0[u] * dc).astype(BF16), sol16))
            kdec = (k32[u] * jnp.exp(gl - g)).astype(BF16)
            trs.append(_dot_tn(kdec, sol16))
        tick()
        for (c, d), rows, q, qo, tr, g, gl in zip(chains, rows_of, q32, qos, trs, bg, glast):
            qp[d, rows, :] = (q * jnp.exp(g) - qo[:, DN_HEAD_DIM:]).astype(BF16)
            kw[d, rows, :] = tr[:, DN_HEAD_DIM:].astype(BF16)
            rr[d, rows, :] = tr[:, :DN_HEAD_DIM]
            eg[d, pl.ds(pl.multiple_of(c * 8, 8), 8), :] = jnp.broadcast_to(jnp.exp(gl), (8, LANES))
            o_ref[rows, :] += qo[:, :DN_HEAD_DIM]

    def scan_step(state, cf, cb, out_f, out_b):
        new = []
        for d, c, c_out in ((0, cf, out_f), (1, cb, out_b)):
            st = state[d]
            st16 = st.astype(BF16)
            rows = chunk_rows(c)
            o_ref[chunk_rows(c_out), :] += _dot(qp[d, rows, :], st16)
            dec = eg[d, pl.ds(pl.multiple_of(c * 8, 8), 1), :]
            new.append(st * dec - _dot(kw[d, rows, :], st16) + rr[d, rows, :])
        return tuple(new)

    n_groups = n_chunks // PREP_UNROLL
    dummy_rows = pl.ds(seq, PREP_UNROLL * CHUNK)
    o_ref[...] = jnp.zeros(o_ref.shape, F32)
    for d in range(2):
        qp[d, dummy_rows, :] = jnp.zeros((PREP_UNROLL * CHUNK, LANES), BF16)
        kw[d, dummy_rows, :] = jnp.zeros((PREP_UNROLL * CHUNK, LANES), BF16)
        rr[d, dummy_rows, :] = jnp.zeros((PREP_UNROLL * CHUNK, LANES), F32)
        eg[d, n_chunks * 8:, :] = jnp.zeros((PREP_UNROLL * 8, LANES), F32)

    def fwd_chunk(g, u):
        return g * PREP_UNROLL + u

    def bwd_chunk(g, u):
        return n_chunks - 1 - g * PREP_UNROLL - u

    def group_fn(i, state):
        chains = ([(fwd_chunk(i, u), 0) for u in range(PREP_UNROLL)]
                  + [(bwd_chunk(i, u), 1) for u in range(PREP_UNROLL)])
        steps = []
        for u in range(PREP_UNROLL):
            cf, cb = fwd_chunk(i - 1, u), bwd_chunk(i - 1, u)
            steps.append((jnp.where(i > 0, cf, n_chunks + u), jnp.where(i > 0, cb, n_chunks + u),
                          jnp.where(i > 0, cf, 0), jnp.where(i > 0, cb, 0)))
        box = [state]

        def tick():
            if steps:
                box[0] = scan_step(box[0], *steps.pop(0))

        prep_group(chains, tick)
        while steps:
            tick()
        return box[0]

    zero = jnp.zeros((DN_HEAD_DIM, DN_HEAD_DIM), F32)
    state = lax.fori_loop(0, n_groups, group_fn, (zero, zero))
    for u in range(PREP_UNROLL):
        cf, cb = fwd_chunk(n_groups - 1, u), bwd_chunk(n_groups - 1, u)
        state = scan_step(state, cf, cb, cf, cb)


def _gdn(dqkv, gf, gt, batch, seq):
    tok = lambda off: pl.BlockSpec((seq, LANES), lambda b, h: (b, h + off))
    dummy = PREP_UNROLL * CHUNK
    scratch = [
        pltpu.VMEM((2, seq + dummy, LANES), BF16),
        pltpu.VMEM((2, seq + dummy, LANES), BF16),
        pltpu.VMEM((2, seq + dummy, LANES), F32),
        pltpu.VMEM((2, (seq + dummy) // CHUNK * 8, LANES), F32),
    ]
    return pl.pallas_call(
        _gdn_kernel,
        grid=(batch, DN_HEADS),
        in_specs=[tok(0), tok(DN_HEADS), tok(2 * DN_HEADS),
                  pl.BlockSpec((seq, LANES), lambda b, h: (b, 0)),
                  pl.BlockSpec((None, N_FEAT, seq), lambda b, h: (h, 0, b))],
        out_specs=pl.BlockSpec((seq, LANES), lambda b, h: (b, h)),
        out_shape=jax.ShapeDtypeStruct((batch * seq, DN_WIDTH), F32),
        scratch_shapes=scratch,
        compiler_params=pltpu.CompilerParams(dimension_semantics=("arbitrary", "arbitrary"),
                                             vmem_limit_bytes=VMEM_LIMIT),
        name="gdn",
    )(dqkv, dqkv, dqkv, gf, gt.reshape(DN_HEADS, N_FEAT, -1))


TM_OUT = 512


def _outproj_kernel(a_ref, d_ref, z_ref, nw_ref, wa_ref, wd_ref, x_ref, o_ref):
    heads = []
    for hh in range(DN_HEADS):
        lanes = slice(hh * DN_HEAD_DIM, (hh + 1) * DN_HEAD_DIM)
        o = d_ref[:, lanes]
        o = o * lax.rsqrt(jnp.mean(o * o, axis=-1, keepdims=True) + EPS) * nw_ref[...]
        heads.append((o * _silu(z_ref[:, lanes])).astype(BF16))
    delta = jnp.concatenate(heads, axis=1)
    o_ref[...] = x_ref[...] + _dot(a_ref[...], wa_ref[...]) + _dot(delta, wd_ref[...])


def _outproj(attn, dn_out, dz, dn_w, w_a, w_d, x2d):
    t = x2d.shape[0]
    tm = TM_OUT
    row = lambda i: (i, 0)
    const = lambda i: (0, 0)
    return pl.pallas_call(
        _outproj_kernel,
        grid=(t // tm,),
        in_specs=[pl.BlockSpec((tm, NA_WIDTH), row), pl.BlockSpec((tm, DN_WIDTH), row),
                  pl.BlockSpec((tm, DN_WIDTH), row), pl.BlockSpec((1, DN_HEAD_DIM), const),
                  pl.BlockSpec(w_a.shape, const), pl.BlockSpec(w_d.shape, const),
                  pl.BlockSpec((tm, D_MODEL), row)],
        out_specs=pl.BlockSpec((tm, D_MODEL), row),
        out_shape=jax.ShapeDtypeStruct((t, D_MODEL), F32),
        compiler_params=pltpu.CompilerParams(dimension_semantics=("arbitrary",),
                                             vmem_limit_bytes=VMEM_LIMIT),
        name="outproj",
    )(attn, dn_out, dz, dn_w, w_a, w_d, x2d)


def _gate_column(p):
    v = jnp.zeros((DN_HEADS, N_FEAT), F32)
    v = v.at[:, F_GC_F].set(p[0].astype(F32)).at[:, F_GC_B].set(p[1].astype(F32))
    return v.reshape(N_GATE, 1)


def _layer(x2d, batch, seq, norm_w, w_in, gain_q, gain_k, rpb, conv_w, a_log, dt_bias, dn_norm_w, w_out):
    n_main = 4 * NA_WIDTH + 4 * DN_WIDTH
    w_main = w_in[:, :n_main].astype(BF16)
    w_gate = w_in[:, n_main:].reshape(-1, N_FEAT, DN_HEADS).transpose(0, 2, 1).reshape(-1, N_GATE)
    w_gate = jnp.pad(w_gate, ((0, 0), (0, LANES - N_GATE))).astype(BF16)
    gq = jnp.tile(gain_q.astype(F32) * (NA_HEAD_DIM ** -0.5 * LOG2E), NA_HEADS)[None, :]
    gk = jnp.tile(gain_k.astype(F32), NA_HEADS)[None, :]
    head_of = np.arange(NA_WIDTH) // NA_HEAD_DIM
    seg = jnp.asarray((head_of[:, None] == head_of[None, :]) / NA_HEAD_DIM, BF16)
    aq, ak, av, az, dqkv, dz, gf, gt = _inproj(
        x2d, norm_w[None, :].astype(F32), w_main, w_gate, gq, gk, seg,
        _gate_column(a_log), _gate_column(dt_bias), conv_w.astype(F32), seq)
    attn = _natten(aq, ak, av, az, _bias_table(rpb), batch, seq)
    dn_out = _gdn(dqkv, gf, gt, batch, seq)
    w_o = w_out.astype(BF16)
    return _outproj(attn, dn_out, dz, dn_norm_w[None, :].astype(F32), w_o[:NA_WIDTH], w_o[NA_WIDTH:], x2d)


def kernel(x, norm_w, w_in, qk_gain_q, qk_gain_k, rpb, conv_w, a_log, dt_bias, dn_norm_w, w_out):
    batch, seq, d = x.shape
    x2d = x.reshape(batch * seq, d)
    for l in range(norm_w.shape[0]):
        x2d = _layer(x2d, batch, seq, norm_w[l], w_in[l], qk_gain_q[l], qk_gain_k[l], rpb[l],
                     conv_w[l], a_log[l], dt_bias[l], dn_norm_w[l], w_out[l])
    return x2d.reshape(batch, seq, d)
```

```python
import functools

import jax
import jax.numpy as jnp
import numpy as np
from jax import lax
from jax.experimental import pallas as pl
from jax.experimental.pallas import tpu as pltpu

F32 = jnp.float32
BF16 = jnp.bfloat16

D_MODEL = 1024
GRID_W = 64
EPS = 1e-6
NA_HEAD_DIM = 64
NA_WIDTH = 512
NA_HEADS = 8
NA_KH = 8
NA_KW = 16
DN_HEAD_DIM = 128
DN_WIDTH = 512
DN_HEADS = 4
DN_CONV = 5
N_GATE = 4 * DN_HEADS

LANES = 128
CHUNK = 128
NEG = -1e30
LOG2E = 1.4426950408889634
VMEM_LIMIT = 56 * 1024 * 1024

N_FEAT = 4
F_BETA_F, F_GC_F, F_BETA_B, F_GC_B = 0, 1, 2, 3


def _dot(a, b):
    return jnp.dot(a, b, preferred_element_type=F32)


def _dot_nt(a, b):
    return lax.dot_general(a, b, (((1,), (1,)), ((), ())), preferred_element_type=F32)


def _dot_tn(a, b):
    return lax.dot_general(a, b, (((0,), (0,)), ((), ())), preferred_element_type=F32)


def _sigmoid(x):
    return 1.0 / (1.0 + jnp.exp(-x))


def _silu(x):
    return x * _sigmoid(x)


TM_IN = 512
HALO_IN = 16
CONV_ROWS = 256
SUBLANES = 8


def _inproj_kernel(tiles_per_seq, xp_ref, x_ref, xn_ref, nw_ref, w_ref, wg_ref, gq_ref, gk_ref, seg_ref,
                   alog_ref, dtb_ref, cw_ref,
                   aq_ref, ak_ref, av_ref, az_ref, dqkv_ref, dz_ref, gf_ref, gt_ref,
                   ph_ref):
    def rmsnorm(x):
        ms = jnp.mean(x * x, axis=-1, keepdims=True)
        return (x * lax.rsqrt(ms + EPS) * nw_ref[...]).astype(BF16)

    h = rmsnorm(x_ref[...])
    tm = h.shape[0]
    h_ext = jnp.concatenate([rmsnorm(xp_ref[...]), h, rmsnorm(xn_ref[...])], axis=0)

    def proj(c0, n):
        return _dot(h, w_ref[:, c0:c0 + n])

    c_dn = 4 * NA_WIDTH

    def conv_blocks(lo, hi):
        for cb in range(lo, hi):
            lanes = slice(cb * LANES, (cb + 1) * LANES)
            for r0 in range(0, tm, CONV_ROWS):
                n_win = CONV_ROWS + 2 * SUBLANES
                win = ph_ref[r0 + HALO_IN - SUBLANES:r0 + HALO_IN + CONV_ROWS + SUBLANES, lanes]
                mid = DN_CONV // 2
                y = win[SUBLANES:SUBLANES + CONV_ROWS, :] * cw_ref[mid:mid + 1, lanes]
                for j in range(DN_CONV):
                    if j != mid:
                        rolled = pltpu.roll(win, (mid - j) % n_win, 0)
                        y = y + rolled[SUBLANES:SUBLANES + CONV_ROWS, :] * cw_ref[j:j + 1, lanes]
                y = _silu(y)
                if cb < 2 * DN_HEADS:
                    scale = DN_HEAD_DIM ** -0.5 if cb < DN_HEADS else 1.0
                    y = y * (lax.rsqrt(jnp.sum(y * y, axis=-1, keepdims=True) + EPS) * scale)
                dqkv_ref[r0:r0 + CONV_ROWS, lanes] = y

    def head_sumsq(t):
        return _dot((t * t).astype(BF16), seg_ref[...])

    raw_tm = _dot(h, wg_ref[...])
    t_q = proj(0, NA_WIDTH)

    raw = raw_tm.T[:N_GATE, :]
    f_idx = lax.broadcasted_iota(jnp.int32, raw.shape, 0) % N_FEAT
    pos = lax.broadcasted_iota(jnp.int32, raw.shape, 1) % CHUNK
    beta = _sigmoid(raw)
    z = raw + dtb_ref[...]
    softplus = jnp.maximum(z, 0.0) + jnp.log1p(jnp.exp(-jnp.abs(z)))
    g = -jnp.exp(alog_ref[...]) * softplus
    cf = g
    cb = g
    s = 1
    while s < CHUNK:
        cf = cf + jnp.where(pos >= s, pltpu.roll(cf, s, 1), 0.0)
        cb = cb + jnp.where(pos < CHUNK - s, pltpu.roll(cb, tm - s, 1), 0.0)
        s *= 2
    feat = jnp.where(f_idx == F_GC_F, cf, jnp.where(f_idx == F_GC_B, cb, beta))
    gt_ref[...] = feat
    gf_ref[...] = jnp.concatenate([feat, jnp.zeros((LANES - N_GATE, tm), F32)], axis=0).T

    p_ext = _dot(h_ext, w_ref[:, c_dn:c_dn + 3 * DN_WIDTH])
    t_in_seq = pl.program_id(0) % tiles_per_seq
    ph_ref[0:HALO_IN, :] = jnp.where(t_in_seq > 0, p_ext[0:HALO_IN, :], 0.0)
    ph_ref[HALO_IN:HALO_IN + tm, :] = p_ext[HALO_IN:HALO_IN + tm, :]
    ph_ref[HALO_IN + tm:, :] = jnp.where(t_in_seq < tiles_per_seq - 1, p_ext[HALO_IN + tm:, :], 0.0)
    t_k = proj(NA_WIDTH, NA_WIDTH)
    m_q = head_sumsq(t_q)
    conv_blocks(0, 4)
    av_ref[...] = proj(2 * NA_WIDTH, NA_WIDTH).astype(BF16)
    m_k = head_sumsq(t_k)
    aq_ref[...] = (t_q * lax.rsqrt(m_q + EPS) * gq_ref[...]).astype(BF16)
    conv_blocks(4, 8)
    az_ref[...] = proj(3 * NA_WIDTH, NA_WIDTH).astype(BF16)
    ak_ref[...] = (t_k * lax.rsqrt(m_k + EPS) * gk_ref[...]).astype(BF16)
    conv_blocks(8, 3 * DN_HEADS)
    dz_ref[...] = proj(c_dn + 3 * DN_WIDTH, DN_WIDTH).astype(BF16)


def _inproj(x2d, nw, w_main, w_gate, gq, gk, seg, alog_vec, dtb_vec, conv_w, seq):
    t = x2d.shape[0]
    tm = TM_IN
    grid = (t // tm,)
    const = lambda i: (0, 0)
    row = lambda i: (i, 0)
    halo_per_tile = tm // HALO_IN
    last_halo = t // HALO_IN - 1
    prev_rows = lambda i: (jnp.maximum(i * halo_per_tile - 1, 0), 0)
    next_rows = lambda i: (jnp.minimum((i + 1) * halo_per_tile, last_halo), 0)
    out_shape = (
        jax.ShapeDtypeStruct((t, NA_WIDTH), BF16),
        jax.ShapeDtypeStruct((t, NA_WIDTH), BF16),
        jax.ShapeDtypeStruct((t, NA_WIDTH), BF16),
        jax.ShapeDtypeStruct((t, NA_WIDTH), BF16),
        jax.ShapeDtypeStruct((t, 3 * DN_WIDTH), F32),
        jax.ShapeDtypeStruct((t, DN_WIDTH), BF16),
        jax.ShapeDtypeStruct((t, LANES), F32),
        jax.ShapeDtypeStruct((N_GATE, t), F32),
    )
    in_specs = [
        pl.BlockSpec((HALO_IN, D_MODEL), prev_rows),
        pl.BlockSpec((tm, D_MODEL), row),
        pl.BlockSpec((HALO_IN, D_MODEL), next_rows),
        pl.BlockSpec((1, D_MODEL), const),
        pl.BlockSpec(w_main.shape, const),
        pl.BlockSpec(w_gate.shape, const),
        pl.BlockSpec((1, NA_WIDTH), const),
        pl.BlockSpec((1, NA_WIDTH), const),
        pl.BlockSpec((NA_WIDTH, NA_WIDTH), const),
        pl.BlockSpec((N_GATE, 1), const),
        pl.BlockSpec((N_GATE, 1), const),
        pl.BlockSpec(conv_w.shape, const),
    ]
    out_specs = (
        pl.BlockSpec((tm, NA_WIDTH), row),
        pl.BlockSpec((tm, NA_WIDTH), row),
        pl.BlockSpec((tm, NA_WIDTH), row),
        pl.BlockSpec((tm, NA_WIDTH), row),
        pl.BlockSpec((tm, 3 * DN_WIDTH), row),
        pl.BlockSpec((tm, DN_WIDTH), row),
        pl.BlockSpec((tm, LANES), row),
        pl.BlockSpec((N_GATE, tm), lambda i: (0, i)),
    )
    return pl.pallas_call(
        functools.partial(_inproj_kernel, seq // tm),
        grid=grid, in_specs=in_specs, out_specs=out_specs, out_shape=out_shape,
        scratch_shapes=[pltpu.VMEM((tm + 2 * HALO_IN, 3 * DN_WIDTH), F32)],
        compiler_params=pltpu.CompilerParams(dimension_semantics=("arbitrary",),
                                             vmem_limit_bytes=VMEM_LIMIT),
        name="inproj",
    )(x2d, x2d, x2d, nw, w_main, w_gate, gq, gk, seg, alog_vec, dtb_vec, conv_w)


NA_GROUP = 4
NA_GW = NA_GROUP * NA_HEAD_DIM
NA_KEYS = NA_KH * GRID_W
N_DR = 2 * NA_KH - 2
NA_ROWS = 8


def _natten_kernel(q_ref, k_ref, v_ref, z_ref, bias_ref, o_ref):
    rows = q_ref.shape[0] // GRID_W
    lane_head = lax.broadcasted_iota(jnp.int32, (GRID_W, NA_GW), 1) // NA_HEAD_DIM

    def rows_fn(i, carry):
        rr = [i * NA_ROWS + u for u in range(NA_ROWS)]
        rs = [jnp.clip(r - NA_KH // 2, 0, rows - NA_KH) for r in rr]
        q0 = [pl.multiple_of(r * GRID_W, GRID_W) for r in rr]
        k0 = [pl.multiple_of(x * GRID_W, GRID_W) for x in rs]
        s = []
        for u in range(NA_ROWS):
            q = q_ref[pl.ds(q0[u], GRID_W), :]
            zero = jnp.zeros_like(q)
            q4 = jnp.concatenate([jnp.where(lane_head == hh, q, zero) for hh in range(NA_GROUP)], axis=0)
            s.append(_dot_nt(q4, k_ref[pl.ds(k0[u], NA_KEYS), :]))
        p, linv = [], []
        for u in range(NA_ROWS):
            dr0 = rs[u] - rr[u] + (NA_KH - 1)
            bias = jnp.concatenate(
                [jnp.concatenate([bias_ref[hh, dr0 + 2 * j] for j in range(NA_KH // 2)], axis=1)
                 for hh in range(NA_GROUP)], axis=0)
            sb = s[u] + bias
            e = jnp.exp2(sb - jnp.max(sb, axis=-1, keepdims=True))
            linv.append(1.0 / jnp.sum(e, axis=-1, keepdims=True))
            p.append(e.astype(BF16))
        o = [_dot(p[u], v_ref[pl.ds(k0[u], NA_KEYS), :]) for u in range(NA_ROWS)]
        for u in range(NA_ROWS):
            on = o[u] * linv[u]
            out = jnp.zeros((GRID_W, NA_GW), F32)
            for hh in range(NA_GROUP):
                out = jnp.where(lane_head == hh, on[hh * GRID_W:(hh + 1) * GRID_W, :], out)
            z = z_ref[pl.ds(q0[u], GRID_W), :].astype(F32)
            o_ref[pl.ds(q0[u], GRID_W), :] = (out * _silu(z)).astype(o_ref.dtype)
        return carry

    lax.fori_loop(0, rows // NA_ROWS, rows_fn, 0)


def _natten(aq, ak, av, az, bias_tab, batch, seq):
    n_groups = NA_HEADS // NA_GROUP
    blk = lambda b, g: (b, g)
    spec = pl.BlockSpec((seq, NA_GW), blk)
    return pl.pallas_call(
        _natten_kernel,
        grid=(batch, n_groups),
        in_specs=[spec, spec, spec, spec,
                  pl.BlockSpec((NA_GROUP, N_DR, GRID_W, LANES), lambda b, g: (g, 0, 0, 0))],
        out_specs=spec,
        out_shape=jax.ShapeDtypeStruct((batch * seq, NA_WIDTH), BF16),
        compiler_params=pltpu.CompilerParams(dimension_semantics=("arbitrary", "arbitrary"),
                                             vmem_limit_bytes=VMEM_LIMIT),
        name="natten",
    )(aq, ak, av, az, bias_tab)


def _bias_table(rpb):
    cols = np.arange(GRID_W)
    cs = np.clip(cols - NA_KW // 2, 0, GRID_W - NA_KW)
    kc = cols[None, :]
    c = cols[:, None]
    valid = (kc >= cs[:, None]) & (kc < cs[:, None] + NA_KW)
    onehot = (kc - c + NA_KW - 1)[None] == np.arange(2 * NA_KW - 1)[:, None, None]
    tab = jnp.einsum('hdk,kcn->hdcn', rpb.astype(F32), jnp.asarray(onehot, F32),
                     precision=lax.Precision.HIGHEST)
    tab = jnp.where(valid[None, None], tab * LOG2E, NEG)
    return jnp.concatenate([tab[:, :-1], tab[:, 1:]], axis=-1).astype(F32)


OUT_TILE = 256
HALF = CHUNK // 2
N_SQUARINGS = 4
PREP_UNROLL = 8


def _triangular_inverse(ms, directions, tick):
    lo = lax.broadcasted_iota(jnp.int32, (HALF, CHUNK), 1) < HALF

    def bf(x):
        return x.astype(BF16)

    def block_diag(x16):
        z = jnp.zeros_like(x16)
        return jnp.concatenate([jnp.where(lo, x16, z), jnp.where(lo, z, x16)], axis=0)

    def below(x16):
        return jnp.concatenate([jnp.zeros_like(x16), x16], axis=0)

    def above(x16):
        return jnp.concatenate([x16, jnp.zeros_like(x16)], axis=0)

    top = [m[:HALF, :] for m in ms]
    bot = [m[HALF:, :] for m in ms]
    pk = [jnp.where(lo, t, b) for t, b in zip(top, bot)]
    m16 = [bf(p) for p in pk]
    e = list(pk)
    pw16 = m16
    for _ in range(N_SQUARINGS):
        pw = [_dot(p, block_diag(p)) for p in pw16]
        tick()
        pw16 = [bf(p) for p in pw]
        ep = [_dot(bf(x), block_diag(p)) for x, p in zip(e, pw16)]
        tick()
        e = [x + p + y for x, p, y in zip(e, pw, ep)]
    res = [m - x + _dot(a, block_diag(bf(x))) for m, a, x in zip(pk, m16, e)]
    tick()
    er = [_dot(bf(x), block_diag(bf(r))) for x, r in zip(e, res)]
    tick()
    e = [x + r + y for x, r, y in zip(e, res, er)]
    off, t = [], []
    for d, tp, bt, x in zip(directions, top, bot, e):
        if d == 0:
            o = jnp.where(lo, bt, 0.0)
            t.append(o + _dot(bf(jnp.where(lo, 0.0, x)), below(bf(o))))
        else:
            o = jnp.where(lo, 0.0, tp)
            t.append(o + _dot(bf(jnp.where(lo, x, 0.0)), above(bf(o))))
        off.append(o)
    tick()
    out = []
    for d, tt, x in zip(directions, t, e):
        if d == 0:
            eoff = tt + _dot(bf(tt), above(bf(jnp.where(lo, x, 0.0))))
            out.append(jnp.concatenate([jnp.where(lo, x, 0.0), jnp.where(lo, eoff, x)], axis=0))
        else:
            eoff = tt + _dot(bf(tt), below(bf(jnp.where(lo, 0.0, x))))
            out.append(jnp.concatenate([jnp.where(lo, x, eoff), jnp.where(lo, 0.0, x)], axis=0))
    tick()
    return out


def _gdn_kernel(qn, kn, vn, gf_ref, gt_ref,
                o_ref,
                qp, kw, rr, eg):
    seq = qn.shape[0]
    n_chunks = seq // CHUNK
    h = pl.program_id(1)

    ii = lax.broadcasted_iota(jnp.int32, (CHUNK, CHUNK), 0)
    jj = lax.broadcasted_iota(jnp.int32, (CHUNK, CHUNK), 1)
    shift = (LANES - h * N_FEAT) % LANES

    dirs = ((F_BETA_F, F_GC_F), (F_BETA_B, F_GC_B))

    def chunk_rows(c):
        return pl.ds(pl.multiple_of(c * CHUNK, CHUNK), CHUNK)

    def prep_group(chains, tick):
        rows_of = [chunk_rows(c) for c, _ in chains]
        k32 = [kn[r, :] for r in rows_of]
        k16 = [k.astype(BF16) for k in k32]
        kk0 = [_dot_nt(k, k) for k in k16]
        tick()
        q32 = [qn[r, :] for r in rows_of]
        qk0 = [_dot_nt(q.astype(BF16), k) for q, k in zip(q32, k16)]
        tick()
        bb, bg, glast, intra, m = [], [], [], [], []
        for u, (c, d) in enumerate(chains):
            fb, fg = dirs[d]
            gfr = pltpu.roll(gf_ref[rows_of[u], :], shift, 1)
            bb.append(jnp.broadcast_to(gfr[:, fb:fb + 1], (CHUNK, LANES)))
            g = jnp.broadcast_to(gfr[:, fg:fg + 1], (CHUNK, LANES))
            bg.append(g)
            glast.append(g[CHUNK - 1:CHUNK, :] if d == 0 else g[0:1, :])
            gcr = gt_ref[fg:fg + 1, rows_of[u]]
            incl = (jj <= ii) if d == 0 else (jj >= ii)
            strict = (jj < ii) if d == 0 else (jj > ii)
            decay = jnp.exp(jnp.where(incl, g - gcr, NEG))
            m.append(jnp.where(strict, -(bb[-1] * kk0[u] * decay), 0.0))
            intra.append((qk0[u] * decay).astype(BF16))
        e = _triangular_inverse(m, [d for _, d in chains], tick)
        sols = []
        for u, (x, b, g) in enumerate(zip(e, bb, bg)):
            rhs = jnp.concatenate([vn[rows_of[u], :] * b, k32[u] * (b * jnp.exp(g))], axis=1)
            sols.append((rhs + _dot(x.astype(BF16), rhs.astype(BF16))).astype(BF16))
        tick()
        qos, trs = [], []
        for u, (sol16, g, gl, a16) in enumerate(zip(sols, bg, glast, intra)):
            qos.append(_dot(a16, sol16))
            kdec = (k32[u] * jnp.exp(gl - g)).astype(BF16)
            trs.append(_dot_tn(kdec, sol16))
        tick()
        for (c, d), rows, q, qo, tr, g, gl in zip(chains, rows_of, q32, qos, trs, bg, glast):
            qp[d, rows, :] = (q * jnp.exp(g) - qo[:, DN_HEAD_DIM:]).astype(BF16)
            kw[d, rows, :] = tr[:, DN_HEAD_DIM:].astype(BF16)
            rr[d, rows, :] = tr[:, :DN_HEAD_DIM]
            eg[d, pl.ds(pl.multiple_of(c * 8, 8), 8), :] = jnp.broadcast_to(jnp.exp(gl), (8, LANES))
            o_ref[rows, :] += qo[:, :DN_HEAD_DIM]

    def scan_step(state, cf, cb, out_f, out_b):
        new = []
        for d, c, c_out in ((0, cf, out_f), (1, cb, out_b)):
            st = state[d]
            st16 = st.astype(BF16)
            rows = chunk_rows(c)
            o_ref[chunk_rows(c_out), :] += _dot(qp[d, rows, :], st16)
            dec = eg[d, pl.ds(pl.multiple_of(c * 8, 8), 1), :]
            new.append(st * dec - _dot(kw[d, rows, :], st16) + rr[d, rows, :])
        return tuple(new)

    n_groups = n_chunks // PREP_UNROLL
    dummy_rows = pl.ds(seq, PREP_UNROLL * CHUNK)
    o_ref[...] = jnp.zeros(o_ref.shape, F32)
    for d in range(2):
        qp[d, dummy_rows, :] = jnp.zeros((PREP_UNROLL * CHUNK, LANES), BF16)
        kw[d, dummy_rows, :] = jnp.zeros((PREP_UNROLL * CHUNK, LANES), BF16)
        rr[d, dummy_rows, :] = jnp.zeros((PREP_UNROLL * CHUNK, LANES), F32)
        eg[d, n_chunks * 8:, :] = jnp.zeros((PREP_UNROLL * 8, LANES), F32)

    def fwd_chunk(g, u):
        return g * PREP_UNROLL + u

    def bwd_chunk(g, u):
        return n_chunks - 1 - g * PREP_UNROLL - u

    def group_fn(i, state):
        chains = ([(fwd_chunk(i, u), 0) for u in range(PREP_UNROLL)]
                  + [(bwd_chunk(i, u), 1) for u in range(PREP_UNROLL)])
        steps = []
        for u in range(PREP_UNROLL):
            cf, cb = fwd_chunk(i - 1, u), bwd_chunk(i - 1, u)
            steps.append((jnp.where(i > 0, cf, n_chunks + u), jnp.where(i > 0, cb, n_chunks + u),
                          jnp.where(i > 0, cf, 0), jnp.where(i > 0, cb, 0)))
        box = [state]

        def tick():
            if steps:
                box[0] = scan_step(box[0], *steps.pop(0))

        prep_group(chains, tick)
        while steps:
            tick()
        return box[0]

    zero = jnp.zeros((DN_HEAD_DIM, DN_HEAD_DIM), F32)
    state = lax.fori_loop(0, n_groups, group_fn, (zero, zero))
    for u in range(PREP_UNROLL):
        cf, cb = fwd_chunk(n_groups - 1, u), bwd_chunk(n_groups - 1, u)
        state = scan_step(state, cf, cb, cf, cb)


def _gdn(dqkv, gf, gt, batch, seq):
    tok = lambda off: pl.BlockSpec((seq, LANES), lambda b, h: (b, h + off))
    dummy = PREP_UNROLL * CHUNK
    scratch = [
        pltpu.VMEM((2, seq + dummy, LANES), BF16),
        pltpu.VMEM((2, seq + dummy, LANES), BF16),
        pltpu.VMEM((2, seq + dummy, LANES), F32),
        pltpu.VMEM((2, (seq + dummy) // CHUNK * 8, LANES), F32),
    ]
    return pl.pallas_call(
        _gdn_kernel,
        grid=(batch, DN_HEADS),
        in_specs=[tok(0), tok(DN_HEADS), tok(2 * DN_HEADS),
                  pl.BlockSpec((seq, LANES), lambda b, h: (b, 0)),
                  pl.BlockSpec((None, N_FEAT, seq), lambda b, h: (h, 0, b))],
        out_specs=pl.BlockSpec((seq, LANES), lambda b, h: (b, h)),
        out_shape=jax.ShapeDtypeStruct((batch * seq, DN_WIDTH), F32),
        scratch_shapes=scratch,
        compiler_params=pltpu.CompilerParams(dimension_semantics=("arbitrary", "arbitrary"),
                                             vmem_limit_bytes=VMEM_LIMIT),
        name="gdn",
    )(dqkv, dqkv, dqkv, gf, gt.reshape(DN_HEADS, N_FEAT, -1))


TM_OUT = 512


def _outproj_kernel(a_ref, d_ref, z_ref, nw_ref, wa_ref, wd_ref, x_ref, o_ref):
    heads = []
    for hh in range(DN_HEADS):
        lanes = slice(hh * DN_HEAD_DIM, (hh + 1) * DN_HEAD_DIM)
        o = d_ref[:, lanes]
        o = o * lax.rsqrt(jnp.mean(o * o, axis=-1, keepdims=True) + EPS) * nw_ref[...]
        heads.append((o * _silu(z_ref[:, lanes].astype(F32))).astype(BF16))
    delta = jnp.concatenate(heads, axis=1)
    o_ref[...] = x_ref[...] + _dot(a_ref[...], wa_ref[...]) + _dot(delta, wd_ref[...])


def _outproj(attn, dn_out, dz, dn_w, w_a, w_d, x2d):
    t = x2d.shape[0]
    tm = TM_OUT
    row = lambda i: (i, 0)
    const = lambda i: (0, 0)
    return pl.pallas_call(
        _outproj_kernel,
        grid=(t // tm,),
        in_specs=[pl.BlockSpec((tm, NA_WIDTH), row), pl.BlockSpec((tm, DN_WIDTH), row),
                  pl.BlockSpec((tm, DN_WIDTH), row), pl.BlockSpec((1, DN_HEAD_DIM), const),
                  pl.BlockSpec(w_a.shape, const), pl.BlockSpec(w_d.shape, const),
                  pl.BlockSpec((tm, D_MODEL), row)],
        out_specs=pl.BlockSpec((tm, D_MODEL), row),
        out_shape=jax.ShapeDtypeStruct((t, D_MODEL), F32),
        compiler_params=pltpu.CompilerParams(dimension_semantics=("arbitrary",),
                                             vmem_limit_bytes=VMEM_LIMIT),
        name="outproj",
    )(attn, dn_out, dz, dn_w, w_a, w_d, x2d)


def _gate_column(p):
    v = jnp.zeros((DN_HEADS, N_FEAT), F32)
    v = v.at[:, F_GC_F].set(p[0].astype(F32)).at[:, F_GC_B].set(p[1].astype(F32))
    return v.reshape(N_GATE, 1)


def _layer(x2d, batch, seq, norm_w, w_in, gain_q, gain_k, rpb, conv_w, a_log, dt_bias, dn_norm_w, w_out):
    n_main = 4 * NA_WIDTH + 4 * DN_WIDTH
    w_main = w_in[:, :n_main].astype(BF16)
    w_gate = w_in[:, n_main:].reshape(-1, N_FEAT, DN_HEADS).transpose(0, 2, 1).reshape(-1, N_GATE)
    w_gate = jnp.pad(w_gate, ((0, 0), (0, LANES - N_GATE))).astype(BF16)
    gq = jnp.tile(gain_q.astype(F32) * (NA_HEAD_DIM ** -0.5 * LOG2E), NA_HEADS)[None, :]
    gk = jnp.tile(gain_k.astype(F32), NA_HEADS)[None, :]
    head_of = np.arange(NA_WIDTH) // NA_HEAD_DIM
    seg = jnp.asarray((head_of[:, None] == head_of[None, :]) / NA_HEAD_DIM, BF16)
    aq, ak, av, az, dqkv, dz, gf, gt = _inproj(
        x2d, norm_w[None, :].astype(F32), w_main, w_gate, gq, gk, seg,
        _gate_column(a_log), _gate_column(dt_bias), conv_w.astype(F32), seq)
    attn = _natten(aq, ak, av, az, _bias_table(rpb), batch, seq)
    dn_out = _gdn(dqkv, gf, gt, batch, seq)
    w_o = w_out.astype(BF16)
    return _outproj(attn, dn_out, dz, dn_norm_w[None, :].astype(F32), w_o[:NA_WIDTH], w_o[NA_WIDTH:], x2d)


def kernel(x, norm_w, w_in, qk_gain_q, qk_gain_k, rpb, conv_w, a_log, dt_bias, dn_norm_w, w_out):
    batch, seq, d = x.shape
    x2d = x.reshape(batch * seq, d)
    for l in range(norm_w.shape[0]):
        x2d = _layer(x2d, batch, seq, norm_w[l], w_in[l], qk_gain_q[l], qk_gain_k[l], rpb[l],
                     conv_w[l], a_log[l], dt_bias[l], dn_norm_w[l], w_out[l])
    return x2d.reshape(batch, seq, d)
```

```python
import functools

import jax
import jax.numpy as jnp
import numpy as np
from jax import lax
from jax.experimental import pallas as pl
from jax.experimental.pallas import tpu as pltpu

F32 = jnp.float32
BF16 = jnp.bfloat16

D_MODEL = 1024
GRID_W = 64
EPS = 1e-6
NA_HEAD_DIM = 64
NA_WIDTH = 512
NA_HEADS = 8
NA_KH = 8
NA_KW = 16
DN_HEAD_DIM = 128
DN_WIDTH = 512
DN_HEADS = 4
DN_CONV = 5
N_GATE = 4 * DN_HEADS

LANES = 128
CHUNK = 128
NEG = -1e30
LOG2E = 1.4426950408889634
VMEM_LIMIT = 56 * 1024 * 1024

N_FEAT = 4
F_BETA_F, F_GC_F, F_BETA_B, F_GC_B = 0, 1, 2, 3


def _dot(a, b):
    return jnp.dot(a, b, preferred_element_type=F32)


def _dot_nt(a, b):
    return lax.dot_general(a, b, (((1,), (1,)), ((), ())), preferred_element_type=F32)


def _dot_tn(a, b):
    return lax.dot_general(a, b, (((0,), (0,)), ((), ())), preferred_element_type=F32)


def _sigmoid(x):
    return 1.0 / (1.0 + jnp.exp(-x))


def _silu(x):
    return x * _sigmoid(x)


TM_IN = 512
HALO_IN = 16
CONV_ROWS = 256
SUBLANES = 8


def _inproj_kernel(tiles_per_seq, xp_ref, x_ref, xn_ref, nw_ref, w_ref, wg_ref, gq_ref, gk_ref, seg_ref,
                   alog_ref, dtb_ref, cw_ref,
                   aq_ref, ak_ref, av_ref, az_ref, dqkv_ref, dz_ref, gf_ref, gt_ref,
                   ph_ref):
    def rmsnorm(x):
        ms = jnp.mean(x * x, axis=-1, keepdims=True)
        return (x * lax.rsqrt(ms + EPS) * nw_ref[...]).astype(BF16)

    h = rmsnorm(x_ref[...])
    tm = h.shape[0]
    h_ext = jnp.concatenate([rmsnorm(xp_ref[...]), h, rmsnorm(xn_ref[...])], axis=0)

    def proj(c0, n):
        return _dot(h, w_ref[:, c0:c0 + n])

    c_dn = 4 * NA_WIDTH

    def conv_blocks(lo, hi):
        for cb in range(lo, hi):
            lanes = slice(cb * LANES, (cb + 1) * LANES)
            for r0 in range(0, tm, CONV_ROWS):
                n_win = CONV_ROWS + 2 * SUBLANES
                win = ph_ref[r0 + HALO_IN - SUBLANES:r0 + HALO_IN + CONV_ROWS + SUBLANES, lanes]
                mid = DN_CONV // 2
                y = win[SUBLANES:SUBLANES + CONV_ROWS, :] * cw_ref[mid:mid + 1, lanes]
                for j in range(DN_CONV):
                    if j != mid:
                        rolled = pltpu.roll(win, (mid - j) % n_win, 0)
                        y = y + rolled[SUBLANES:SUBLANES + CONV_ROWS, :] * cw_ref[j:j + 1, lanes]
                y = _silu(y)
                if cb < 2 * DN_HEADS:
                    scale = DN_HEAD_DIM ** -0.5 if cb < DN_HEADS else 1.0
                    y = y * (lax.rsqrt(jnp.sum(y * y, axis=-1, keepdims=True) + EPS) * scale)
                dqkv_ref[r0:r0 + CONV_ROWS, lanes] = y

    def head_sumsq(t):
        return _dot((t * t).astype(BF16), seg_ref[...])

    raw_tm = _dot(h, wg_ref[...])
    t_q = proj(0, NA_WIDTH)

    raw = raw_tm.T[:N_GATE, :]
    f_idx = lax.broadcasted_iota(jnp.int32, raw.shape, 0) % N_FEAT
    pos = lax.broadcasted_iota(jnp.int32, raw.shape, 1) % CHUNK
    beta = _sigmoid(raw)
    z = raw + dtb_ref[...]
    softplus = jnp.maximum(z, 0.0) + jnp.log1p(jnp.exp(-jnp.abs(z)))
    g = -jnp.exp(alog_ref[...]) * softplus
    cf = g
    cb = g
    s = 1
    while s < CHUNK:
        cf = cf + jnp.where(pos >= s, pltpu.roll(cf, s, 1), 0.0)
        cb = cb + jnp.where(pos < CHUNK - s, pltpu.roll(cb, tm - s, 1), 0.0)
        s *= 2
    feat = jnp.where(f_idx == F_GC_F, cf, jnp.where(f_idx == F_GC_B, cb, beta))
    gt_ref[...] = feat
    gf_ref[...] = jnp.concatenate([feat, jnp.zeros((LANES - N_GATE, tm), F32)], axis=0).T

    t_in_seq = pl.program_id(0) % tiles_per_seq

    def project_dn(part):
        cols = slice(part * DN_WIDTH, (part + 1) * DN_WIDTH)
        p_ext = _dot(h_ext, w_ref[:, c_dn + part * DN_WIDTH:c_dn + (part + 1) * DN_WIDTH])
        ph_ref[0:HALO_IN, cols] = jnp.where(t_in_seq > 0, p_ext[0:HALO_IN, :], 0.0)
        ph_ref[HALO_IN:HALO_IN + tm, cols] = p_ext[HALO_IN:HALO_IN + tm, :]
        ph_ref[HALO_IN + tm:, cols] = jnp.where(t_in_seq < tiles_per_seq - 1, p_ext[HALO_IN + tm:, :], 0.0)

    project_dn(0)
    t_k = proj(NA_WIDTH, NA_WIDTH)
    m_q = head_sumsq(t_q)
    conv_blocks(0, DN_HEADS)
    project_dn(1)
    av_ref[...] = proj(2 * NA_WIDTH, NA_WIDTH).astype(BF16)
    m_k = head_sumsq(t_k)
    aq_ref[...] = (t_q * lax.rsqrt(m_q + EPS) * gq_ref[...]).astype(BF16)
    conv_blocks(DN_HEADS, 2 * DN_HEADS)
    project_dn(2)
    az_ref[...] = proj(3 * NA_WIDTH, NA_WIDTH).astype(BF16)
    ak_ref[...] = (t_k * lax.rsqrt(m_k + EPS) * gk_ref[...]).astype(BF16)
    conv_blocks(2 * DN_HEADS, 3 * DN_HEADS)
    dz_ref[...] = proj(c_dn + 3 * DN_WIDTH, DN_WIDTH).astype(BF16)


def _inproj(x2d, nw, w_main, w_gate, gq, gk, seg, alog_vec, dtb_vec, conv_w, seq):
    t = x2d.shape[0]
    tm = TM_IN
    grid = (t // tm,)
    const = lambda i: (0, 0)
    row = lambda i: (i, 0)
    halo_per_tile = tm // HALO_IN
    last_halo = t // HALO_IN - 1
    prev_rows = lambda i: (jnp.maximum(i * halo_per_tile - 1, 0), 0)
    next_rows = lambda i: (jnp.minimum((i + 1) * halo_per_tile, last_halo), 0)
    out_shape = (
        jax.ShapeDtypeStruct((t, NA_WIDTH), BF16),
        jax.ShapeDtypeStruct((t, NA_WIDTH), BF16),
        jax.ShapeDtypeStruct((t, NA_WIDTH), BF16),
        jax.ShapeDtypeStruct((t, NA_WIDTH), BF16),
        jax.ShapeDtypeStruct((t, 3 * DN_WIDTH), F32),
        jax.ShapeDtypeStruct((t, DN_WIDTH), BF16),
        jax.ShapeDtypeStruct((t, LANES), F32),
        jax.ShapeDtypeStruct((N_GATE, t), F32),
    )
    in_specs = [
        pl.BlockSpec((HALO_IN, D_MODEL), prev_rows),
        pl.BlockSpec((tm, D_MODEL), row),
        pl.BlockSpec((HALO_IN, D_MODEL), next_rows),
        pl.BlockSpec((1, D_MODEL), const),
        pl.BlockSpec(w_main.shape, const),
        pl.BlockSpec(w_gate.shape, const),
        pl.BlockSpec((1, NA_WIDTH), const),
        pl.BlockSpec((1, NA_WIDTH), const),
        pl.BlockSpec((NA_WIDTH, NA_WIDTH), const),
        pl.BlockSpec((N_GATE, 1), const),
        pl.BlockSpec((N_GATE, 1), const),
        pl.BlockSpec(conv_w.shape, const),
    ]
    out_specs = (
        pl.BlockSpec((tm, NA_WIDTH), row),
        pl.BlockSpec((tm, NA_WIDTH), row),
        pl.BlockSpec((tm, NA_WIDTH), row),
        pl.BlockSpec((tm, NA_WIDTH), row),
        pl.BlockSpec((tm, 3 * DN_WIDTH), row),
        pl.BlockSpec((tm, DN_WIDTH), row),
        pl.BlockSpec((tm, LANES), row),
        pl.BlockSpec((N_GATE, tm), lambda i: (0, i)),
    )
    return pl.pallas_call(
        functools.partial(_inproj_kernel, seq // tm),
        grid=grid, in_specs=in_specs, out_specs=out_specs, out_shape=out_shape,
        scratch_shapes=[pltpu.VMEM((tm + 2 * HALO_IN, 3 * DN_WIDTH), F32)],
        compiler_params=pltpu.CompilerParams(dimension_semantics=("arbitrary",),
                                             vmem_limit_bytes=VMEM_LIMIT),
        name="inproj",
    )(x2d, x2d, x2d, nw, w_main, w_gate, gq, gk, seg, alog_vec, dtb_vec, conv_w)


NA_GROUP = 4
NA_GW = NA_GROUP * NA_HEAD_DIM
NA_KEYS = NA_KH * GRID_W
N_DR = 2 * NA_KH - 2
NA_ROWS = 8


def _natten_kernel(q_ref, k_ref, v_ref, z_ref, bias_ref, o_ref):
    rows = q_ref.shape[0] // GRID_W
    lane_head = lax.broadcasted_iota(jnp.int32, (GRID_W, NA_GW), 1) // NA_HEAD_DIM

    def rows_fn(i, carry):
        rr = [i * NA_ROWS + u for u in range(NA_ROWS)]
        rs = [jnp.clip(r - NA_KH // 2, 0, rows - NA_KH) for r in rr]
        q0 = [pl.multiple_of(r * GRID_W, GRID_W) for r in rr]
        k0 = [pl.multiple_of(x * GRID_W, GRID_W) for x in rs]
        s = []
        for u in range(NA_ROWS):
            q = q_ref[pl.ds(q0[u], GRID_W), :]
            zero = jnp.zeros_like(q)
            q4 = jnp.concatenate([jnp.where(lane_head == hh, q, zero) for hh in range(NA_GROUP)], axis=0)
            s.append(_dot_nt(q4, k_ref[pl.ds(k0[u], NA_KEYS), :]))
        p, linv = [], []
        for u in range(NA_ROWS):
            dr0 = rs[u] - rr[u] + (NA_KH - 1)
            bias = jnp.concatenate(
                [jnp.concatenate([bias_ref[hh, dr0 + 2 * j] for j in range(NA_KH // 2)], axis=1)
                 for hh in range(NA_GROUP)], axis=0)
            sb = s[u] + bias
            e = jnp.exp2(sb - jnp.max(sb, axis=-1, keepdims=True))
            linv.append(1.0 / jnp.sum(e, axis=-1, keepdims=True))
            p.append(e.astype(BF16))
        o = [_dot(p[u], v_ref[pl.ds(k0[u], NA_KEYS), :]) for u in range(NA_ROWS)]
        for u in range(NA_ROWS):
            on = o[u] * linv[u]
            out = jnp.zeros((GRID_W, NA_GW), F32)
            for hh in range(NA_GROUP):
                out = jnp.where(lane_head == hh, on[hh * GRID_W:(hh + 1) * GRID_W, :], out)
            z = z_ref[pl.ds(q0[u], GRID_W), :].astype(F32)
            o_ref[pl.ds(q0[u], GRID_W), :] = (out * _silu(z)).astype(o_ref.dtype)
        return carry

    lax.fori_loop(0, rows // NA_ROWS, rows_fn, 0)


def _natten(aq, ak, av, az, bias_tab, batch, seq):
    n_groups = NA_HEADS // NA_GROUP
    blk = lambda b, g: (b, g)
    spec = pl.BlockSpec((seq, NA_GW), blk)
    return pl.pallas_call(
        _natten_kernel,
        grid=(batch, n_groups),
        in_specs=[spec, spec, spec, spec,
                  pl.BlockSpec((NA_GROUP, N_DR, GRID_W, LANES), lambda b, g: (g, 0, 0, 0))],
        out_specs=spec,
        out_shape=jax.ShapeDtypeStruct((batch * seq, NA_WIDTH), BF16),
        compiler_params=pltpu.CompilerParams(dimension_semantics=("arbitrary", "arbitrary"),
                                             vmem_limit_bytes=VMEM_LIMIT),
        name="natten",
    )(aq, ak, av, az, bias_tab)


def _bias_table(rpb):
    cols = np.arange(GRID_W)
    cs = np.clip(cols - NA_KW // 2, 0, GRID_W - NA_KW)
    kc = cols[None, :]
    c = cols[:, None]
    valid = (kc >= cs[:, None]) & (kc < cs[:, None] + NA_KW)
    onehot = (kc - c + NA_KW - 1)[None] == np.arange(2 * NA_KW - 1)[:, None, None]
    tab = jnp.einsum('hdk,kcn->hdcn', rpb.astype(F32), jnp.asarray(onehot, F32),
                     precision=lax.Precision.HIGHEST)
    tab = jnp.where(valid[None, None], tab * LOG2E, NEG)
    return jnp.concatenate([tab[:, :-1], tab[:, 1:]], axis=-1).astype(F32)


OUT_TILE = 256
HALF = CHUNK // 2
N_SQUARINGS = 4
PREP_UNROLL = 8


def _triangular_inverse(ms, directions, tick):
    lo = lax.broadcasted_iota(jnp.int32, (HALF, CHUNK), 1) < HALF

    def bf(x):
        return x.astype(BF16)

    def block_diag(x16):
        z = jnp.zeros_like(x16)
        return jnp.concatenate([jnp.where(lo, x16, z), jnp.where(lo, z, x16)], axis=0)

    def below(x16):
        return jnp.concatenate([jnp.zeros_like(x16), x16], axis=0)

    def above(x16):
        return jnp.concatenate([x16, jnp.zeros_like(x16)], axis=0)

    top = [m[:HALF, :] for m in ms]
    bot = [m[HALF:, :] for m in ms]
    pk = [jnp.where(lo, t, b) for t, b in zip(top, bot)]
    m16 = [bf(p) for p in pk]
    e = list(pk)
    pw16 = m16
    for _ in range(N_SQUARINGS):
        pw = [_dot(p, block_diag(p)) for p in pw16]
        tick()
        pw16 = [bf(p) for p in pw]
        ep = [_dot(bf(x), block_diag(p)) for x, p in zip(e, pw16)]
        tick()
        e = [x + p + y for x, p, y in zip(e, pw, ep)]
    res = [m - x + _dot(a, block_diag(bf(x))) for m, a, x in zip(pk, m16, e)]
    tick()
    er = [_dot(bf(x), block_diag(bf(r))) for x, r in zip(e, res)]
    tick()
    e = [x + r + y for x, r, y in zip(e, res, er)]
    off, t = [], []
    for d, tp, bt, x in zip(directions, top, bot, e):
        if d == 0:
            o = jnp.where(lo, bt, 0.0)
            t.append(o + _dot(bf(jnp.where(lo, 0.0, x)), below(bf(o))))
        else:
            o = jnp.where(lo, 0.0, tp)
            t.append(o + _dot(bf(jnp.where(lo, x, 0.0)), above(bf(o))))
        off.append(o)
    tick()
    out = []
    for d, tt, x in zip(directions, t, e):
        if d == 0:
            eoff = tt + _dot(bf(tt), above(bf(jnp.where(lo, x, 0.0))))
            out.append(jnp.concatenate([jnp.where(lo, x, 0.0), jnp.where(lo, eoff, x)], axis=0))
        else:
            eoff = tt + _dot(bf(tt), below(bf(jnp.where(lo, 0.0, x))))
            out.append(jnp.concatenate([jnp.where(lo, x, eoff), jnp.where(lo, 0.0, x)], axis=0))
    tick()
    return out


def _gdn_kernel(qn, kn, vn, gf_ref, gt_ref,
                o_ref,
                qp, kw, rr, eg):
    seq = qn.shape[0]
    n_chunks = seq // CHUNK
    h = pl.program_id(1)

    ii = lax.broadcasted_iota(jnp.int32, (CHUNK, CHUNK), 0)
    jj = lax.broadcasted_iota(jnp.int32, (CHUNK, CHUNK), 1)
    shift = (LANES - h * N_FEAT) % LANES

    dirs = ((F_BETA_F, F_GC_F), (F_BETA_B, F_GC_B))

    def chunk_rows(c):
        return pl.ds(pl.multiple_of(c * CHUNK, CHUNK), CHUNK)

    def prep_group(chains, tick):
        rows_of = [chunk_rows(c) for c, _ in chains]
        k32 = [kn[r, :] for r in rows_of]
        k16 = [k.astype(BF16) for k in k32]
        kk0 = [_dot_nt(k, k) for k in k16]
        tick()
        q32 = [qn[r, :] for r in rows_of]
        qk0 = [_dot_nt(q.astype(BF16), k) for q, k in zip(q32, k16)]
        tick()
        bb, bg, glast, intra, m = [], [], [], [], []
        for u, (c, d) in enumerate(chains):
            fb, fg = dirs[d]
            gfr = pltpu.roll(gf_ref[rows_of[u], :], shift, 1)
            bb.append(jnp.broadcast_to(gfr[:, fb:fb + 1], (CHUNK, LANES)))
            g = jnp.broadcast_to(gfr[:, fg:fg + 1], (CHUNK, LANES))
            bg.append(g)
            glast.append(g[CHUNK - 1:CHUNK, :] if d == 0 else g[0:1, :])
            gcr = gt_ref[fg:fg + 1, rows_of[u]]
            incl = (jj <= ii) if d == 0 else (jj >= ii)
            strict = (jj < ii) if d == 0 else (jj > ii)
            decay = jnp.exp(jnp.where(incl, g - gcr, NEG))
            m.append(jnp.where(strict, -(bb[-1] * kk0[u] * decay), 0.0))
            intra.append((qk0[u] * decay).astype(BF16))
        e = _triangular_inverse(m, [d for _, d in chains], tick)
        sols = []
        for u, (x, b, g) in enumerate(zip(e, bb, bg)):
            rhs = jnp.concatenate([vn[rows_of[u], :] * b, k32[u] * (b * jnp.exp(g))], axis=1)
            sols.append((rhs + _dot(x.astype(BF16), rhs.astype(BF16))).astype(BF16))
        tick()
        qos, trs = [], []
        for u, (sol16, g, gl, a16) in enumerate(zip(sols, bg, glast, intra)):
            qos.append(_dot(a16, sol16))
            kdec = (k32[u] * jnp.exp(gl - g)).astype(BF16)
            trs.append(_dot_tn(kdec, sol16))
        tick()
        for (c, d), rows, q, qo, tr, g, gl in zip(chains, rows_of, q32, qos, trs, bg, glast):
            qp[d, rows, :] = (q * jnp.exp(g) - qo[:, DN_HEAD_DIM:]).astype(BF16)
            kw[d, rows, :] = tr[:, DN_HEAD_DIM:].astype(BF16)
            rr[d, rows, :] = tr[:, :DN_HEAD_DIM]
            eg[d, pl.ds(pl.multiple_of(c * 8, 8), 8), :] = jnp.broadcast_to(jnp.exp(gl), (8, LANES))
            o_ref[rows, :] += qo[:, :DN_HEAD_DIM]

    def scan_step(state, cf, cb, out_f, out_b):
        new = []
        for d, c, c_out in ((0, cf, out_f), (1, cb, out_b)):
            st = state[d]
            st16 = st.astype(BF16)
            rows = chunk_rows(c)
            o_ref[chunk_rows(c_out), :] += _dot(qp[d, rows, :], st16)
            dec = eg[d, pl.ds(pl.multiple_of(c * 8, 8), 1), :]
            new.append(st * dec - _dot(kw[d, rows, :], st16) + rr[d, rows, :])
        return tuple(new)

    n_groups = n_chunks // PREP_UNROLL
    dummy_rows = pl.ds(seq, PREP_UNROLL * CHUNK)
    o_ref[...] = jnp.zeros(o_ref.shape, F32)
    for d in range(2):
        qp[d, dummy_rows, :] = jnp.zeros((PREP_UNROLL * CHUNK, LANES), BF16)
        kw[d, dummy_rows, :] = jnp.zeros((PREP_UNROLL * CHUNK, LANES), BF16)
        rr[d, dummy_rows, :] = jnp.zeros((PREP_UNROLL * CHUNK, LANES), F32)
        eg[d, n_chunks * 8:, :] = jnp.zeros((PREP_UNROLL * 8, LANES), F32)

    def fwd_chunk(g, u):
        return g * PREP_UNROLL + u

    def bwd_chunk(g, u):
        return n_chunks - 1 - g * PREP_UNROLL - u

    def group_fn(i, state):
        chains = ([(fwd_chunk(i, u), 0) for u in range(PREP_UNROLL)]
                  + [(bwd_chunk(i, u), 1) for u in range(PREP_UNROLL)])
        steps = []
        for u in range(PREP_UNROLL):
            cf, cb = fwd_chunk(i - 1, u), bwd_chunk(i - 1, u)
            steps.append((jnp.where(i > 0, cf, n_chunks + u), jnp.where(i > 0, cb, n_chunks + u),
                          jnp.where(i > 0, cf, 0), jnp.where(i > 0, cb, 0)))
        box = [state]

        def tick():
            if steps:
                box[0] = scan_step(box[0], *steps.pop(0))

        prep_group(chains, tick)
        while steps:
            tick()
        return box[0]

    zero = jnp.zeros((DN_HEAD_DIM, DN_HEAD_DIM), F32)
    state = lax.fori_loop(0, n_groups, group_fn, (zero, zero))
    for u in range(PREP_UNROLL):
        cf, cb = fwd_chunk(n_groups - 1, u), bwd_chunk(n_groups - 1, u)
        state = scan_step(state, cf, cb, cf, cb)


def _gdn(dqkv, gf, gt, batch, seq):
    tok = lambda off: pl.BlockSpec((seq, LANES), lambda b, h: (b, h + off))
    dummy = PREP_UNROLL * CHUNK
    scratch = [
        pltpu.VMEM((2, seq + dummy, LANES), BF16),
        pltpu.VMEM((2, seq + dummy, LANES), BF16),
        pltpu.VMEM((2, seq + dummy, LANES), F32),
        pltpu.VMEM((2, (seq + dummy) // CHUNK * 8, LANES), F32),
    ]
    return pl.pallas_call(
        _gdn_kernel,
        grid=(batch, DN_HEADS),
        in_specs=[tok(0), tok(DN_HEADS), tok(2 * DN_HEADS),
                  pl.BlockSpec((seq, LANES), lambda b, h: (b, 0)),
                  pl.BlockSpec((None, N_FEAT, seq), lambda b, h: (h, 0, b))],
        out_specs=pl.BlockSpec((seq, LANES), lambda b, h: (b, h)),
        out_shape=jax.ShapeDtypeStruct((batch * seq, DN_WIDTH), F32),
        scratch_shapes=scratch,
        compiler_params=pltpu.CompilerParams(dimension_semantics=("arbitrary", "arbitrary"),
                                             vmem_limit_bytes=VMEM_LIMIT),
        name="gdn",
    )(dqkv, dqkv, dqkv, gf, gt.reshape(DN_HEADS, N_FEAT, -1))


TM_OUT = 512


def _outproj_kernel(a_ref, d_ref, z_ref, nw_ref, wa_ref, wd_ref, x_ref, o_ref):
    heads = []
    for hh in range(DN_HEADS):
        lanes = slice(hh * DN_HEAD_DIM, (hh + 1) * DN_HEAD_DIM)
        o = d_ref[:, lanes]
        o = o * lax.rsqrt(jnp.mean(o * o, axis=-1, keepdims=True) + EPS) * nw_ref[...]
        heads.append((o * _silu(z_ref[:, lanes].astype(F32))).astype(BF16))
    delta = jnp.concatenate(heads, axis=1)
    o_ref[...] = x_ref[...] + _dot(a_ref[...], wa_ref[...]) + _dot(delta, wd_ref[...])


def _outproj(attn, dn_out, dz, dn_w, w_a, w_d, x2d):
    t = x2d.shape[0]
    tm = TM_OUT
    row = lambda i: (i, 0)
    const = lambda i: (0, 0)
    return pl.pallas_call(
        _outproj_kernel,
        grid=(t // tm,),
        in_specs=[pl.BlockSpec((tm, NA_WIDTH), row), pl.BlockSpec((tm, DN_WIDTH), row),
                  pl.BlockSpec((tm, DN_WIDTH), row), pl.BlockSpec((1, DN_HEAD_DIM), const),
                  pl.BlockSpec(w_a.shape, const), pl.BlockSpec(w_d.shape, const),
                  pl.BlockSpec((tm, D_MODEL), row)],
        out_specs=pl.BlockSpec((tm, D_MODEL), row),
        out_shape=jax.ShapeDtypeStruct((t, D_MODEL), F32),
        compiler_params=pltpu.CompilerParams(dimension_semantics=("arbitrary",),
                                             vmem_limit_bytes=VMEM_LIMIT),
        name="outproj",
    )(attn, dn_out, dz, dn_w, w_a, w_d, x2d)


def _gate_column(p):
    v = jnp.zeros((DN_HEADS, N_FEAT), F32)
    v = v.at[:, F_GC_F].set(p[0].astype(F32)).at[:, F_GC_B].set(p[1].astype(F32))
    return v.reshape(N_GATE, 1)


def _layer(x2d, batch, seq, norm_w, w_in, gain_q, gain_k, rpb, conv_w, a_log, dt_bias, dn_norm_w, w_out):
    n_main = 4 * NA_WIDTH + 4 * DN_WIDTH
    w_main = w_in[:, :n_main].astype(BF16)
    w_gate = w_in[:, n_main:].reshape(-1, N_FEAT, DN_HEADS).transpose(0, 2, 1).reshape(-1, N_GATE)
    w_gate = jnp.pad(w_gate, ((0, 0), (0, LANES - N_GATE))).astype(BF16)
    gq = jnp.tile(gain_q.astype(F32) * (NA_HEAD_DIM ** -0.5 * LOG2E), NA_HEADS)[None, :]
    gk = jnp.tile(gain_k.astype(F32), NA_HEADS)[None, :]
    head_of = np.arange(NA_WIDTH) // NA_HEAD_DIM
    seg = jnp.asarray((head_of[:, None] == head_of[None, :]) / NA_HEAD_DIM, BF16)
    aq, ak, av, az, dqkv, dz, gf, gt = _inproj(
        x2d, norm_w[None, :].astype(F32), w_main, w_gate, gq, gk, seg,
        _gate_column(a_log), _gate_column(dt_bias), conv_w.astype(F32), seq)
    attn = _natten(aq, ak, av, az, _bias_table(rpb), batch, seq)
    dn_out = _gdn(dqkv, gf, gt, batch, seq)
    w_o = w_out.astype(BF16)
    return _outproj(attn, dn_out, dz, dn_norm_w[None, :].astype(F32), w_o[:NA_WIDTH], w_o[NA_WIDTH:], x2d)


def kernel(x, norm_w, w_in, qk_gain_q, qk_gain_k, rpb, conv_w, a_log, dt_bias, dn_norm_w, w_out):
    batch, seq, d = x.shape
    x2d = x.reshape(batch * seq, d)
    for l in range(norm_w.shape[0]):
        x2d = _layer(x2d, batch, seq, norm_w[l], w_in[l], qk_gain_q[l], qk_gain_k[l], rpb[l],
                     conv_w[l], a_log[l], dt_bias[l], dn_norm_w[l], w_out[l])
    return x2d.reshape(batch, seq, d)
```
